```python
import jax
import jax.numpy as jnp
from jax import lax
import numpy as np

D_MODEL = 1024
BATCH = 8
SEQ = 2048
DEPTH = 4

CHUNK = 64
N_EVEN = (DEPTH + 1) // 2
N_ODD = DEPTH // 2
D_FF = 2816
MIX_WIDTH = D_MODEL
RMS_EPS = 1e-6
LN_EPS = 1e-5

GSU_BLOCK = 128
A_GROUPS = 4
D_A = MIX_WIDTH // 2
A_GROUP_DIM = D_A // A_GROUPS
D_B = MIX_WIDTH - D_A
B_HEAD = 64
B_HEADS = D_B // B_HEAD
LORA_W = 64
LORA_A = 64
LORA_G = 128
LNX_EPS = 64e-5
P_B = 3 * D_B + LORA_W + LORA_A + LORA_G
P_EVEN = 2 * D_A + P_B
C_HEADS = 8
Q_LORA = 256
KV_LORA = 128
QK_NOPE = 64
QK_ROPE = 32
V_HEAD = 64
D_C = C_HEADS * V_HEAD
ROPE_THETA = 10000.0
Q_BLOCK = 128
ATTN_SCALE = (QK_NOPE + QK_ROPE) ** -0.5
D_D = MIX_WIDTH - D_C
CONV_K = 31
P_ODD = Q_LORA + KV_LORA + QK_ROPE + 2 * D_D

kernel_name = 'hybrid_chunk_causal_encoder'


def rmsnorm(x, g):
    xf = x.astype(jnp.float32)
    y = xf * lax.rsqrt(jnp.mean(xf * xf, axis=-1, keepdims=True) + RMS_EPS)
    return (y * g.astype(jnp.float32)).astype(x.dtype)


def layernorm(x, g, b, eps):
    xf = x.astype(jnp.float32)
    mu = jnp.mean(xf, axis=-1, keepdims=True)
    var = jnp.mean(jnp.square(xf - mu), axis=-1, keepdims=True)
    y = (xf - mu) * lax.rsqrt(var + eps) * g.astype(jnp.float32) + b.astype(jnp.float32)
    return y.astype(x.dtype)


def swiglu_ffn(x, w_in, w_out):
    gate, up = jnp.split(x @ w_in, 2, axis=-1)
    return (jax.nn.silu(gate) * up) @ w_out


def token_shift(z):
    return jnp.pad(z[:, :-1], ((0, 0), (1, 0), (0, 0)))


def rope_tables(s):
    inv_freq = ROPE_THETA ** (-jnp.arange(0, QK_ROPE, 2, dtype=jnp.float32) / QK_ROPE)
    ang = jnp.arange(s, dtype=jnp.float32)[:, None] * inv_freq[None, :]
    return jnp.cos(ang), jnp.sin(ang)


def apply_rope(x, cos, sin):
    x1, x2 = jnp.split(x, 2, axis=-1)
    c = cos[None, :, None, :]
    s = sin[None, :, None, :]
    return jnp.concatenate([x1 * c - x2 * s, x1 * s + x2 * c], axis=-1).astype(x.dtype)


def gsu_mixer(za, ws, bs, ln_g, ln_b):
    u, v = jnp.split(jax.nn.gelu(za), 2, axis=-1)
    v = layernorm(v, ln_g, ln_b, LN_EPS)
    b, s, _ = v.shape
    vb = v.reshape(b, s // GSU_BLOCK, GSU_BLOCK, A_GROUPS, A_GROUP_DIM)
    pos_chunk = jnp.arange(GSU_BLOCK) // CHUNK
    mask = pos_chunk[:, None] >= pos_chunk[None, :]
    w = jnp.where(mask[None], ws, jnp.zeros_like(ws))
    mixed = jnp.einsum('gij,bnjgc->bnigc', w, vb) + bs.T[None, None, :, :, None]
    return u * mixed.reshape(b, s, D_A)


def rwkv7_mixer(zb, shift_mu, decay_w0, decay_up, iclr_a0, iclr_up, gate_up,
                k_k, k_a, r_k, lnx_g, lnx_b):
    b, s, _ = zb.shape
    z = zb + shift_mu * (token_shift(zb) - zb)
    r, k, v, xw, xa, xg = jnp.split(
        z, [D_B, 2 * D_B, 3 * D_B, 3 * D_B + LORA_W, 3 * D_B + LORA_W + LORA_A], axis=-1)
    logw = -jax.nn.softplus(-(decay_w0 + jnp.tanh(xw) @ decay_up)) - 0.5
    decay = jnp.exp(-jnp.exp(logw.astype(jnp.float32)))
    a = jax.nn.sigmoid(iclr_a0 + xa @ iclr_up)
    g = jax.nn.sigmoid(xg) @ gate_up
    heads = lambda t: t.reshape(b, s, B_HEADS, B_HEAD)
    kk = heads(k * k_k).astype(jnp.float32)
    kk = kk / jnp.maximum(jnp.linalg.norm(kk, axis=-1, keepdims=True), 1e-12)
    k = k * (1.0 + (a - 1.0) * k_a)
    rh, kh, vh, ah = heads(r), heads(k), heads(v), heads(a)
    tm = lambda t: jnp.moveaxis(t, 1, 0).astype(jnp.float32)
    xs = (tm(rh), tm(heads(decay)), tm(kh), tm(vh), tm(-kk), tm(kk * ah))

    def step(state, inp):
        r_t, w_t, k_t, v_t, a_t, b_t = inp
        sa = jnp.einsum('bhvk,bhk->bhv', state, a_t)
        state = (state * w_t[:, :, None, :] + sa[..., None] * b_t[:, :, None, :]
                 + v_t[..., None] * k_t[:, :, None, :])
        return state, jnp.einsum('bhvk,bhk->bhv', state, r_t)

    state0 = jnp.zeros((b, B_HEADS, B_HEAD, B_HEAD), jnp.float32)
    _, y = lax.scan(step, state0, xs)
    y = jnp.moveaxis(y, 0, 1)
    y = layernorm(y, lnx_g.reshape(B_HEADS, B_HEAD), lnx_b.reshape(B_HEADS, B_HEAD), LNX_EPS)
    bonus = jnp.sum(rh * kh * r_k, axis=-1, keepdims=True) * vh
    return ((y + bonus).reshape(b, s, D_B) * g).astype(zb.dtype)


def mla_mixer(cq, ckv, kr, q_norm, wq_up, kv_norm, wkv_up, cos, sin):
    b, s, _ = cq.shape
    q = (rmsnorm(cq, q_norm) @ wq_up).reshape(b, s, C_HEADS, QK_NOPE + QK_ROPE)
    q_nope = q[..., :QK_NOPE]
    q_rope = apply_rope(q[..., QK_NOPE:], cos, sin)
    kv = (rmsnorm(ckv, kv_norm) @ wkv_up).reshape(b, s, C_HEADS, QK_NOPE + V_HEAD)
    k_nope, v = kv[..., :QK_NOPE], kv[..., QK_NOPE:]
    k_rope = apply_rope(kr[:, :, None, :], cos, sin)[:, :, 0]
    n_blk = s // Q_BLOCK

    def to_blocks(t):
        return jnp.moveaxis(t.reshape(b, n_blk, Q_BLOCK, C_HEADS, t.shape[-1]), 1, 0)

    key_chunk = jnp.arange(s) // CHUNK

    def attend(args):
        qn, qr, blk = args
        sc = (jnp.einsum('bqhd,bkhd->bhqk', qn, k_nope)
              + jnp.einsum('bqhd,bkd->bhqk', qr, k_rope))
        sc = sc.astype(jnp.float32) * ATTN_SCALE
        q_chunk = (blk * Q_BLOCK + jnp.arange(Q_BLOCK)) // CHUNK
        sc = jnp.where(key_chunk[None, :] <= q_chunk[:, None], sc, -jnp.inf)
        p = jax.nn.softmax(sc, axis=-1).astype(v.dtype)
        return jnp.einsum('bhqk,bkhd->bqhd', p, v)

    o = lax.map(attend, (to_blocks(q_nope), to_blocks(q_rope), jnp.arange(n_blk)))
    return jnp.moveaxis(o, 0, 1).reshape(b, s, D_C)


def conv_mixer(zd, conv_w, conv_b, ln_g, ln_b):
    h = jax.nn.glu(zd, axis=-1)
    h = lax.conv_general_dilated(h, conv_w[:, None, :], window_strides=(1,),
                                 padding=[(CONV_K - 1, 0)],
                                 dimension_numbers=('NWC', 'WIO', 'NWC'),
                                 feature_group_count=D_D) + conv_b
    return jax.nn.silu(layernorm(h, ln_g, ln_b, LN_EPS))


def setup_inputs(seed: int = 0) -> dict:
    key = jax.random.key(seed)
    ks = iter(jax.random.split(key, 48))
    f32 = jnp.float32

    def nrm(shape, scale):
        return jax.random.normal(next(ks), shape, f32) * scale

    def gain(shape):
        return 1.0 + nrm(shape, 0.01)

    return {
        'x': nrm((BATCH, SEQ, D_MODEL), 1.0),
        'norm_ffn1': gain((DEPTH, D_MODEL)),
        'ffn1_in': nrm((DEPTH, D_MODEL, 2 * D_FF), D_MODEL ** -0.5),
        'ffn1_out': nrm((DEPTH, D_FF, D_MODEL), D_FF ** -0.5),
        'norm_mix': gain((DEPTH, D_MODEL)),
        'norm_ffn2': gain((DEPTH, D_MODEL)),
        'ffn2_in': nrm((DEPTH, D_MODEL, 2 * D_FF), D_MODEL ** -0.5),
        'ffn2_out': nrm((DEPTH, D_FF, D_MODEL), D_FF ** -0.5),
        'even_w_in': nrm((N_EVEN, D_MODEL, P_EVEN), D_MODEL ** -0.5),
        'even_w_out': nrm((N_EVEN, MIX_WIDTH, D_MODEL), MIX_WIDTH ** -0.5),
        'gsu_ws': nrm((N_EVEN, A_GROUPS, GSU_BLOCK, GSU_BLOCK), 0.5 * GSU_BLOCK ** -0.5),
        'gsu_bs': 1.0 + nrm((N_EVEN, A_GROUPS, GSU_BLOCK), 0.1),
        'gsu_ln_g': gain((N_EVEN, D_A)),
        'gsu_ln_b': nrm((N_EVEN, D_A), 0.01),
        'shift_mu': jax.random.uniform(next(ks), (N_EVEN, P_B), f32),
        'decay_w0': jnp.linspace(-6.0, -1.0, D_B, dtype=f32)[None, :] + nrm((N_EVEN, D_B), 0.1),
        'decay_up': nrm((N_EVEN, LORA_W, D_B), 0.1),
        'iclr_a0': nrm((N_EVEN, D_B), 0.1),
        'iclr_up': nrm((N_EVEN, LORA_A, D_B), 0.5 * LORA_A ** -0.5),
        'gate_up': nrm((N_EVEN, LORA_G, D_B), LORA_G ** -0.5),
        'k_k': 0.85 + nrm((N_EVEN, D_B), 0.05),
        'k_a': 1.0 + nrm((N_EVEN, D_B), 0.05),
        'r_k': nrm((N_EVEN, B_HEADS, B_HEAD), 0.1),
        'lnx_g': gain((N_EVEN, D_B)),
        'lnx_b': nrm((N_EVEN, D_B), 0.01),
        'odd_w_in': nrm((N_ODD, D_MODEL, P_ODD), D_MODEL ** -0.5),
        'odd_w_out': nrm((N_ODD, MIX_WIDTH, D_MODEL), MIX_WIDTH ** -0.5),
        'q_norm': gain((N_ODD, Q_LORA)),
        'wq_up': nrm((N_ODD, Q_LORA, C_HEADS * (QK_NOPE + QK_ROPE)), Q_LORA ** -0.5),
        'kv_norm': gain((N_ODD, KV_LORA)),
        'wkv_up': nrm((N_ODD, KV_LORA, C_HEADS * (QK_NOPE + V_HEAD)), KV_LORA ** -0.5),
        'conv_w': nrm((N_ODD, CONV_K, D_D), CONV_K ** -0.5),
        'conv_b': nrm((N_ODD, D_D), 0.01),
        'conv_ln_g': gain((N_ODD, D_D)),
        'conv_ln_b': nrm((N_ODD, D_D), 0.01),
        'final_norm': gain((D_MODEL,)),
    }


def reference(x, norm_ffn1, ffn1_in, ffn1_out, norm_mix, norm_ffn2, ffn2_in, ffn2_out,
              even_w_in, even_w_out, gsu_ws, gsu_bs, gsu_ln_g, gsu_ln_b,
              shift_mu, decay_w0, decay_up, iclr_a0, iclr_up, gate_up,
              k_k, k_a, r_k, lnx_g, lnx_b,
              odd_w_in, odd_w_out, q_norm, wq_up, kv_norm, wkv_up,
              conv_w, conv_b, conv_ln_g, conv_ln_b, final_norm):
    cos, sin = rope_tables(x.shape[1])
    for layer in range(DEPTH):
        x = x + 0.5 * swiglu_ffn(rmsnorm(x, norm_ffn1[layer]), ffn1_in[layer], ffn1_out[layer])
        h = rmsnorm(x, norm_mix[layer])
        if layer % 2 == 0:
            e = layer // 2
            z = h @ even_w_in[e]
            ya = gsu_mixer(z[..., :2 * D_A], gsu_ws[e], gsu_bs[e], gsu_ln_g[e], gsu_ln_b[e])
            yb = rwkv7_mixer(z[..., 2 * D_A:], shift_mu[e], decay_w0[e], decay_up[e],
                             iclr_a0[e], iclr_up[e], gate_up[e], k_k[e], k_a[e], r_k[e],
                             lnx_g[e], lnx_b[e])
            x = x + jnp.concatenate([ya, yb], axis=-1) @ even_w_out[e]
        else:
            o = layer // 2
            z = h @ odd_w_in[o]
            cq = z[..., :Q_LORA]
            ckv = z[..., Q_LORA:Q_LORA + KV_LORA]
            kr = z[..., Q_LORA + KV_LORA:Q_LORA + KV_LORA + QK_ROPE]
            zd = z[..., Q_LORA + KV_LORA + QK_ROPE:]
            yc = mla_mixer(cq, ckv, kr, q_norm[o], wq_up[o], kv_norm[o], wkv_up[o], cos, sin)
            yd = conv_mixer(zd, conv_w[o], conv_b[o], conv_ln_g[o], conv_ln_b[o])
            x = x + jnp.concatenate([yc, yd], axis=-1) @ odd_w_out[o]
        x = x + 0.5 * swiglu_ffn(rmsnorm(x, norm_ffn2[layer]), ffn2_in[layer], ffn2_out[layer])
    return rmsnorm(x, final_norm)
```

```python
import functools
import math

import jax
import jax.numpy as jnp
from jax import lax
from jax.experimental import pallas as pl
from jax.experimental.pallas import tpu as pltpu

F32 = jnp.float32
BF16 = jnp.bfloat16

LANES = 128
SUBLANES = 8
VMEM_LIMIT_BYTES = 56 * 1024 * 1024

CHUNK = 64
RMS_EPS = 1e-6
LN_EPS = 1e-5
LNX_EPS = 64e-5
GSU_BLOCK = 128
A_GROUPS = 4
B_HEAD = 64
LORA_W = 64
LORA_A = 64
LORA_G = 128
C_HEADS = 8
Q_LORA = 256
KV_LORA = 128
QK_NOPE = 64
QK_ROPE = 32
V_HEAD = 64
ROPE_THETA = 10000.0
ATTN_SCALE = (QK_NOPE + QK_ROPE) ** -0.5
CONV_K = 31
HEAD_SLOT = 128
ROPE_LO = QK_NOPE
ROPE_HALF = QK_ROPE // 2


def _params(*sem):
    return pltpu.CompilerParams(dimension_semantics=sem, vmem_limit_bytes=VMEM_LIMIT_BYTES)


def _dot(a, b):
    return jnp.dot(a, b, preferred_element_type=F32)


def _dot_nt(a, b):
    return lax.dot_general(a, b, (((1,), (1,)), ((), ())), preferred_element_type=F32)


def _dot_tn(a, b):
    return lax.dot_general(a, b, (((0,), (0,)), ((), ())), preferred_element_type=F32)


def _split_hi_lo(x):
    hi = x.astype(BF16)
    lo = (x - hi.astype(F32)).astype(BF16)
    return hi, lo


def _dot_exact_rhs(x, m):
    hi, lo = _split_hi_lo(x)
    return _dot(hi, m) + _dot(lo, m)


def _dot_exact_lhs(m, x):
    hi, lo = _split_hi_lo(x)
    return _dot(m, hi) + _dot(m, lo)


def _rms(x, g):
    ms = jnp.mean(x * x, axis=-1, keepdims=True)
    return x * lax.rsqrt(ms + RMS_EPS) * g


def _layernorm(x, g, b, eps):
    mu = jnp.mean(x, axis=-1, keepdims=True)
    xc = x - mu
    var = jnp.mean(xc * xc, axis=-1, keepdims=True)
    return xc * lax.rsqrt(var + eps) * g + b


def _sigmoid(x):
    return 1.0 / (1.0 + jnp.exp(-x))


def _full(shape):
    n = len(shape)
    return pl.BlockSpec(shape, lambda *_: (0,) * n)


FFN_CHUNK = 256


def _ffn_kernel(*refs, d_ff, final):
    if final:
        x_ref, g_ref, win_ref, wout_ref, fin_ref, o_ref, acc_ref = refs
    else:
        x_ref, g_ref, win_ref, wout_ref, o_ref, acc_ref = refs
    x = x_ref[...]
    h = _rms(x, g_ref[...]).astype(BF16)
    for j in range(d_ff // FFN_CHUNK):
        lo = j * FFN_CHUNK
        gate = _dot(h, win_ref[:, lo:lo + FFN_CHUNK])
        up = _dot(h, win_ref[:, d_ff + lo:d_ff + lo + FFN_CHUNK])
        act = (gate * _sigmoid(gate) * up).astype(BF16)
        part = _dot(act, wout_ref[lo:lo + FFN_CHUNK, :])
        if j == 0:
            acc_ref[...] = part
        else:
            acc_ref[...] += part
    y = x + 0.5 * acc_ref[...]
    if final:
        y = _rms(y, fin_ref[...])
    o_ref[...] = y


def _ffn(x, g, w_in, w_out, fin_g=None, tm=512):
    t, d = x.shape
    d_ff = w_out.shape[0]
    final = fin_g is not None
    in_specs = [pl.BlockSpec((tm, d), lambda i: (i, 0)), _full((1, d)),
                _full(w_in.shape), _full(w_out.shape)]
    args = [x, g.reshape(1, d), w_in, w_out]
    if final:
        in_specs.append(_full((1, d)))
        args.append(fin_g.reshape(1, d))
    return pl.pallas_call(
        functools.partial(_ffn_kernel, d_ff=d_ff, final=final),
        grid=(t // tm,),
        in_specs=in_specs,
        out_specs=pl.BlockSpec((tm, d), lambda i: (i, 0)),
        out_shape=jax.ShapeDtypeStruct((t, d), F32),
        scratch_shapes=[pltpu.VMEM((tm, d), F32)],
        compiler_params=_params("parallel"),
        name="ffn_final" if final else "ffn",
    )(*args)


def _inproj_kernel(x_ref, g_ref, w_ref, *o_refs, splits):
    h = _rms(x_ref[...], g_ref[...]).astype(BF16)
    lo = 0
    for o_ref, width in zip(o_refs, splits):
        o_ref[...] = _dot(h, w_ref[:, lo:lo + width])
        lo += width


def _inproj(x, g, w, splits, tm=512, name="inproj"):
    t, d = x.shape
    return pl.pallas_call(
        functools.partial(_inproj_kernel, splits=splits),
        grid=(t // tm,),
        in_specs=[pl.BlockSpec((tm, d), lambda i: (i, 0)), _full((1, d)), _full(w.shape)],
        out_specs=[pl.BlockSpec((tm, s), lambda i: (i, 0)) for s in splits],
        out_shape=[jax.ShapeDtypeStruct((t, s), F32) for s in splits],
        compiler_params=_params("parallel"),
        name=name,
    )(x, g.reshape(1, d), w)


def _outproj_kernel(x_ref, ya_ref, yb_ref, w_ref, o_ref):
    da = ya_ref.shape[1]
    acc = _dot(ya_ref[...].astype(BF16), w_ref[:da, :])
    acc += _dot(yb_ref[...].astype(BF16), w_ref[da:, :])
    o_ref[...] = x_ref[...] + acc


def _outproj(x, ya, yb, w, tm=512):
    t, d = x.shape
    return pl.pallas_call(
        _outproj_kernel,
        grid=(t // tm,),
        in_specs=[pl.BlockSpec((tm, d), lambda i: (i, 0)),
                  pl.BlockSpec((tm, ya.shape[1]), lambda i: (i, 0)),
                  pl.BlockSpec((tm, yb.shape[1]), lambda i: (i, 0)),
                  _full(w.shape)],
        out_specs=pl.BlockSpec((tm, d), lambda i: (i, 0)),
        out_shape=jax.ShapeDtypeStruct((t, d), F32),
        compiler_params=_params("parallel"),
        name="outproj",
    )(x, ya, yb, w)


def _gelu_tanh(x):
    c = math.sqrt(2.0 / math.pi)
    return 0.5 * x * (1.0 + jnp.tanh(c * (x + 0.044715 * (x * x * x))))


def _gsu_kernel(za_ref, ws_ref, bias_ref, lng_ref, lnb_ref, o_ref):
    d_a = o_ref.shape[1]
    gd = d_a // A_GROUPS
    g = _gelu_tanh(za_ref[...])
    u = g[:, :d_a]
    v = _layernorm(g[:, d_a:], lng_ref[...], lnb_ref[...], LN_EPS).astype(BF16)
    row = lax.broadcasted_iota(jnp.int32, (GSU_BLOCK, GSU_BLOCK), 0) // CHUNK
    col = lax.broadcasted_iota(jnp.int32, (GSU_BLOCK, GSU_BLOCK), 1) // CHUNK
    keep = row >= col
    for gi in range(A_GROUPS):
        w = jnp.where(keep, ws_ref[gi], 0.0).astype(BF16)
        for blk in range(za_ref.shape[0] // GSU_BLOCK):
            rs = slice(blk * GSU_BLOCK, (blk + 1) * GSU_BLOCK)
            cs = slice(gi * gd, (gi + 1) * gd)
            mixed = _dot(w, v[rs, cs]) + bias_ref[gi]
            o_ref[rs, cs] = u[rs, cs] * mixed


def _gsu(za, ws, bs, ln_g, ln_b, tb=512):
    t, two_da = za.shape
    d_a = two_da // 2
    bias = jnp.broadcast_to(bs[:, :, None], (A_GROUPS, GSU_BLOCK, d_a // A_GROUPS))
    return pl.pallas_call(
        _gsu_kernel,
        grid=(t // tb,),
        in_specs=[pl.BlockSpec((tb, two_da), lambda i: (i, 0)), _full(ws.shape), _full(bias.shape),
                  _full((1, d_a)), _full((1, d_a))],
        out_specs=pl.BlockSpec((tb, d_a), lambda i: (i, 0)),
        out_shape=jax.ShapeDtypeStruct((t, d_a), F32),
        compiler_params=_params("parallel"),
        name="gsu",
    )(za, ws, bias, ln_g.reshape(1, d_a), ln_b.reshape(1, d_a))


def _rwkv_prep_kernel(zb_ref, mu_ref, w0_ref, a0_ref, kk_ref, ka_ref, rk_ref, wup_ref, aup_ref, gup_ref,
                      ones_ref, r_o, k_o, v_o, ew_o, a_o, b_o, g_o, bonus_o, buf_ref, *, d_b):
    tt = zb_ref.shape[0]
    i = pl.program_id(1)

    @pl.when(i == 0)
    def _():
        buf_ref[0:SUBLANES, :] = jnp.zeros((SUBLANES, zb_ref.shape[1]), F32)

    zb = zb_ref[...]
    buf_ref[SUBLANES:SUBLANES + tt, :] = zb
    prev = buf_ref[SUBLANES - 1:SUBLANES - 1 + tt, :]
    z = zb + mu_ref[...] * (prev - zb)
    buf_ref[0:SUBLANES, :] = zb[tt - SUBLANES:, :]

    r = z[:, 0:d_b]
    k = z[:, d_b:2 * d_b]
    v = z[:, 2 * d_b:3 * d_b]
    xwa = z[:, 3 * d_b:3 * d_b + LORA_W + LORA_A]
    xg = z[:, 3 * d_b + LORA_W + LORA_A:]
    lane = lax.broadcasted_iota(jnp.int32, xwa.shape, 1)
    xwa = jnp.where(lane < LORA_W, jnp.tanh(xwa), xwa).astype(BF16)
    pre = w0_ref[...] + _dot(xwa, wup_ref[...])
    ew = _sigmoid(pre) * math.exp(-0.5)
    a = _sigmoid(a0_ref[...] + _dot(xwa, aup_ref[...]))
    g = _dot(_sigmoid(xg).astype(BF16), gup_ref[...])
    kk = k * kk_ref[...]
    ss = _dot_exact_rhs(kk * kk, ones_ref[...])
    kk = kk / jnp.maximum(jnp.sqrt(ss), 1e-12)
    k2 = k * (1.0 + (a - 1.0) * ka_ref[...])
    bonus = _dot_exact_rhs(r * k2 * rk_ref[...], ones_ref[...]) * v
    r_o[...] = r
    k_o[...] = k2
    v_o[...] = v
    ew_o[...] = ew
    a_o[...] = -kk
    b_o[...] = kk * a
    g_o[...] = g
    bonus_o[...] = bonus


def _head_ones(d, head):
    idx = jnp.arange(d) // head
    return (idx[:, None] == idx[None, :]).astype(BF16)


def _rwkv_prep(zb, batch, shift_mu, decay_w0, decay_up, iclr_a0, iclr_up, gate_up, k_k, k_a, r_k, tt=256):
    t, p_b = zb.shape
    d_b = decay_w0.shape[0]
    s = t // batch
    nt = s // tt
    zeros = jnp.zeros((LORA_W, d_b), F32)
    wup = jnp.concatenate([decay_up, zeros], axis=0).astype(BF16)
    aup = jnp.concatenate([zeros, iclr_up], axis=0).astype(BF16)
    row = lambda a: a.reshape(1, -1)
    tile = pl.BlockSpec((tt, d_b), lambda b, i: (b * nt + i, 0))
    return pl.pallas_call(
        functools.partial(_rwkv_prep_kernel, d_b=d_b),
        grid=(batch, nt),
        in_specs=[pl.BlockSpec((tt, p_b), lambda b, i: (b * nt + i, 0)), _full((1, p_b)),
                  _full((1, d_b)), _full((1, d_b)), _full((1, d_b)), _full((1, d_b)), _full((1, d_b)),
                  _full(wup.shape), _full(aup.shape), _full((LORA_G, d_b)), _full((d_b, d_b))],
        out_specs=[tile] * 8,
        out_shape=[jax.ShapeDtypeStruct((t, d_b), F32)] * 8,
        scratch_shapes=[pltpu.VMEM((tt + SUBLANES, p_b), F32)],
        compiler_params=_params("parallel", "arbitrary"),
        name="rwkv_prep",
    )(zb, row(shift_mu), row(decay_w0), row(iclr_a0), row(k_k), row(k_a), row(r_k),
      wup, aup, gate_up.astype(BF16), _head_ones(d_b, B_HEAD))


def _rwkv_scan_kernel(r_ref, k_ref, v_ref, ew_ref, a_ref, b_ref, g_ref, bonus_ref, lng_ref, lnb_ref,
                      ones_ref, o_ref, h_ref, y_ref):
    c, d_b = r_ref.shape
    heads = d_b // B_HEAD

    @pl.when(pl.program_id(1) == 0)
    def _():
        h_ref[...] = jnp.zeros(h_ref.shape, F32)

    ti = lax.broadcasted_iota(jnp.int32, (c, c), 0)
    si = lax.broadcasted_iota(jnp.int32, (c, c), 1)
    tril = (si <= ti).astype(BF16)
    strict = si < ti
    incl = si <= ti
    eye = (ti == si).astype(F32)

    ew = ew_ref[...]
    cum = _dot_exact_lhs(tril, ew)
    total = cum[c - 1:c, :]
    dec_in = jnp.exp(-cum)
    dec_ex = jnp.exp(ew - cum)
    grow = jnp.exp(cum)
    tail = jnp.exp(cum - total)
    a_t = a_ref[...] * dec_ex
    r_t = r_ref[...] * dec_in
    b_t = b_ref[...] * grow
    k_t = k_ref[...] * grow
    b_h = b_ref[...] * tail
    k_h = k_ref[...] * tail
    g_end = jnp.exp(-total)
    v = v_ref[...]

    for h in range(heads):
        hs = slice(h * B_HEAD, (h + 1) * B_HEAD)
        vh = v[:, hs].astype(BF16)
        lhs = jnp.concatenate([a_t[:, hs], r_t[:, hs]], axis=0).astype(BF16)
        rhs = jnp.concatenate([b_t[:, hs], k_t[:, hs]], axis=0).astype(BF16)
        p = _dot_nt(lhs, rhs)
        n_ab = jnp.where(strict, p[:c, :c], 0.0)
        a_ak = jnp.where(strict, p[:c, c:], 0.0)
        m_rb = jnp.where(incl, p[c:, :c], 0.0)
        m_rk = jnp.where(incl, p[c:, c:], 0.0)
        x = jnp.concatenate([a_t[:, hs], _dot(a_ak.astype(BF16), vh)], axis=1)
        nk = n_ab
        steps = int(math.log2(c))
        for it in range(steps):
            nkb = nk.astype(BF16)
            x = x + _dot(nkb, x.astype(BF16))
            if it + 1 < steps:
                nk = _dot(nkb, nkb)
        xb = x.astype(BF16)
        ry = _dot(m_rb.astype(BF16), xb) + jnp.concatenate(
            [r_t[:, hs], _dot(m_rk.astype(BF16), vh)], axis=1)
        pg = _dot_tn(b_h[:, hs].astype(BF16), xb) + jnp.concatenate(
            [eye * g_end[:, hs], _dot_tn(k_h[:, hs].astype(BF16), vh)], axis=1)
        h0 = h_ref[h]
        h0b = h0.astype(BF16)
        y_ref[:, hs] = _dot(ry[:, :B_HEAD].astype(BF16), h0b) + ry[:, B_HEAD:]
        h_ref[h] = _dot(pg[:, :B_HEAD].astype(BF16), h0b) + pg[:, B_HEAD:]

    y = y_ref[...]
    inv = 1.0 / B_HEAD
    mu = _dot_exact_rhs(y, ones_ref[...]) * inv
    yc = y - mu
    var = _dot_exact_rhs(yc * yc, ones_ref[...]) * inv
    yn = yc * lax.rsqrt(var + LNX_EPS) * lng_ref[...] + lnb_ref[...]
    o_ref[...] = (yn + bonus_ref[...]) * g_ref[...]


def _rwkv_scan(r, k, v, ew, a, b, g, bonus, lnx_g, lnx_b, batch):
    t, d_b = r.shape
    s = t // batch
    nc = s // CHUNK
    tile = pl.BlockSpec((CHUNK, d_b), lambda bi, i: (bi * nc + i, 0))
    return pl.pallas_call(
        _rwkv_scan_kernel,
        grid=(batch, nc),
        in_specs=[tile] * 8 + [_full((1, d_b)), _full((1, d_b)), _full((d_b, d_b))],
        out_specs=tile,
        out_shape=jax.ShapeDtypeStruct((t, d_b), F32),
        scratch_shapes=[pltpu.VMEM((d_b // B_HEAD, B_HEAD, B_HEAD), F32), pltpu.VMEM((CHUNK, d_b), F32)],
        compiler_params=_params("parallel", "arbitrary"),
        name="rwkv_scan",
    )(r, k, v, ew, a, b, g, bonus, lnx_g.reshape(1, d_b), lnx_b.reshape(1, d_b), _head_ones(d_b, B_HEAD))


def _rope_table_kernel(cos_ref, sin_ref):
    rows = cos_ref.shape[0]
    pos = (pl.program_id(0) * rows + lax.broadcasted_iota(jnp.int32, cos_ref.shape, 0)).astype(F32)
    lane = lax.broadcasted_iota(jnp.int32, cos_ref.shape, 1)
    in_rope = (lane >= ROPE_LO) & (lane < ROPE_LO + QK_ROPE)
    second = lane >= ROPE_LO + ROPE_HALF
    idx = jnp.where(second, lane - ROPE_LO - ROPE_HALF, lane - ROPE_LO).astype(F32)
    inv_freq = jnp.exp(idx * (-2.0 * math.log(ROPE_THETA) / QK_ROPE))
    ang = pos * inv_freq
    cos_ref[...] = jnp.where(in_rope, jnp.cos(ang), 1.0)
    sin_ref[...] = jnp.where(in_rope, jnp.where(second, jnp.sin(ang), -jnp.sin(ang)), 0.0)


def _rope_tables(s, rows=256):
    spec = pl.BlockSpec((rows, HEAD_SLOT), lambda i: (i, 0))
    return pl.pallas_call(
        _rope_table_kernel,
        grid=(s // rows,),
        out_specs=[spec, spec],
        out_shape=[jax.ShapeDtypeStruct((s, HEAD_SLOT), F32)] * 2,
        compiler_params=_params("parallel"),
        name="rope_tables",
    )()


def _rope_slot(x, cos_t, sin_t):
    lane = lax.broadcasted_iota(jnp.int32, x.shape, 1)
    partner = jnp.where(lane < ROPE_LO + ROPE_HALF,
                        pltpu.roll(x, HEAD_SLOT - ROPE_HALF, 1), pltpu.roll(x, ROPE_HALF, 1))
    return x * cos_t + partner * sin_t


def _mla_prep_kernel(cq_ref, ckv_ref, kr_ref, cos_ref, sin_ref, qn_ref, kvn_ref, wq_ref, wk_ref, wv_ref,
                     q_o, k_o, v_o):
    cos_t = cos_ref[...]
    sin_t = sin_ref[...]
    cqn = _rms(cq_ref[...], qn_ref[...]).astype(BF16)
    ckvn = _rms(ckv_ref[...], kvn_ref[...]).astype(BF16)
    q = _dot(cqn, wq_ref[...])
    kn = _dot(ckvn, wk_ref[...])
    v_o[...] = _dot(ckvn, wv_ref[...]).astype(BF16)
    krot = _rope_slot(kr_ref[...], cos_t, sin_t)
    for h in range(C_HEADS):
        hs = slice(h * HEAD_SLOT, (h + 1) * HEAD_SLOT)
        q_o[0, h] = (_rope_slot(q[:, hs], cos_t, sin_t) * ATTN_SCALE).astype(BF16)
        k_o[0, h] = (kn[:, hs] + krot).astype(BF16)


def _mla_prep(cq, ckv, kr, cos_t, sin_t, q_norm, kv_norm, wq, wk, wv, batch, tm=256):
    t = cq.shape[0]
    s = t // batch
    nt = s // tm
    rows = lambda w: pl.BlockSpec((tm, w), lambda b, i: (b * nt + i, 0))
    tab = pl.BlockSpec((tm, HEAD_SLOT), lambda b, i: (i, 0))
    qk_spec = pl.BlockSpec((1, C_HEADS, tm, HEAD_SLOT), lambda b, i: (b, 0, i, 0))
    qk_shape = jax.ShapeDtypeStruct((batch, C_HEADS, s, HEAD_SLOT), BF16)
    return pl.pallas_call(
        _mla_prep_kernel,
        grid=(batch, nt),
        in_specs=[rows(Q_LORA), rows(KV_LORA), rows(HEAD_SLOT), tab, tab,
                  _full((1, Q_LORA)), _full((1, KV_LORA)), _full(wq.shape), _full(wk.shape), _full(wv.shape)],
        out_specs=[qk_spec, qk_spec, rows(C_HEADS * V_HEAD)],
        out_shape=[qk_shape, qk_shape, jax.ShapeDtypeStruct((t, C_HEADS * V_HEAD), BF16)],
        compiler_params=_params("parallel", "parallel"),
        name="mla_prep",
    )(cq, ckv, kr, cos_t, sin_t, q_norm.reshape(1, -1), kv_norm.reshape(1, -1), wq, wk, wv)


ATTN_TQ = 256


def _attn_kernel(q_ref, k_ref, v_ref, o_ref):
    tq = q_ref.shape[2]
    qi = pl.program_id(2)
    lane = lax.broadcasted_iota(jnp.int32, (tq, 2 * V_HEAD), 1)
    row_c = lax.broadcasted_iota(jnp.int32, (tq, tq), 0) // CHUNK
    col_c = lax.broadcasted_iota(jnp.int32, (tq, tq), 1) // CHUNK
    diag_keep = col_c <= row_c
    outs = []
    for hh in range(2):
        q = q_ref[0, hh]

        def block(j, carry, masked):
            m, l, acc = carry
            start = pl.multiple_of(j * tq, tq)
            kb = k_ref[0, hh, pl.ds(start, tq), :]
            vb = v_ref[pl.ds(start, tq), :]
            sc = _dot_nt(q, kb)
            if masked:
                sc = jnp.where(diag_keep, sc, -jnp.inf)
            m_new = jnp.maximum(m, jnp.max(sc, axis=-1, keepdims=True))
            alpha = jnp.exp(m - m_new)
            p = jnp.exp(sc - m_new)
            l = alpha * l + jnp.sum(p, axis=-1, keepdims=True)
            acc = alpha * acc + _dot(p.astype(BF16), vb)
            return m_new, l, acc

        init = (jnp.full((tq, 1), -jnp.inf, F32), jnp.zeros((tq, 1), F32), jnp.zeros((tq, 2 * V_HEAD), F32))
        carry = lax.fori_loop(0, qi, functools.partial(block, masked=False), init)
        m, l, acc = block(qi, carry, True)
        outs.append(acc / l)
    o_ref[...] = jnp.where(lane < V_HEAD, outs[0], outs[1])


def _attention(q, k, v, batch):
    _, heads, s, slot = q.shape
    nq = s // ATTN_TQ
    t = batch * s
    return pl.pallas_call(
        _attn_kernel,
        grid=(batch, heads // 2, nq),
        in_specs=[pl.BlockSpec((1, 2, ATTN_TQ, slot), lambda b, p, i: (b, p, i, 0)),
                  pl.BlockSpec((1, 2, s, slot), lambda b, p, i: (b, p, 0, 0)),
                  pl.BlockSpec((s, 2 * V_HEAD), lambda b, p, i: (b, p))],
        out_specs=pl.BlockSpec((ATTN_TQ, 2 * V_HEAD), lambda b, p, i: (b * nq + i, p)),
        out_shape=jax.ShapeDtypeStruct((t, heads * V_HEAD), F32),
        compiler_params=_params("parallel", "parallel", "arbitrary"),
        name="mla_attention",
    )(q, k, v)


CONV_HALO = 32


def _conv_kernel(zd_ref, w_ref, b_ref, lng_ref, lnb_ref, o_ref, buf_ref):
    tt, d_d = o_ref.shape

    @pl.when(pl.program_id(1) == 0)
    def _():
        buf_ref[0:CONV_HALO, :] = jnp.zeros((CONV_HALO, d_d), F32)

    zd = zd_ref[...]
    buf_ref[CONV_HALO:CONV_HALO + tt, :] = zd[:, :d_d] * _sigmoid(zd[:, d_d:])
    acc = jnp.zeros((tt, d_d), F32) + b_ref[...]
    first = CONV_HALO - (CONV_K - 1)
    for j in range(CONV_K):
        acc = acc + w_ref[j:j + 1, :] * buf_ref[first + j:first + j + tt, :]
    buf_ref[0:CONV_HALO, :] = buf_ref[tt:tt + CONV_HALO, :]
    y = _layernorm(acc, lng_ref[...], lnb_ref[...], LN_EPS)
    o_ref[...] = y * _sigmoid(y)


def _conv(zd, conv_w, conv_b, ln_g, ln_b, batch, tt=256):
    t, two_d = zd.shape
    d_d = two_d // 2
    nt = (t // batch) // tt
    return pl.pallas_call(
        _conv_kernel,
        grid=(batch, nt),
        in_specs=[pl.BlockSpec((tt, two_d), lambda b, i: (b * nt + i, 0)), _full(conv_w.shape),
                  _full((1, d_d)), _full((1, d_d)), _full((1, d_d))],
        out_specs=pl.BlockSpec((tt, d_d), lambda b, i: (b * nt + i, 0)),
        out_shape=jax.ShapeDtypeStruct((t, d_d), F32),
        scratch_shapes=[pltpu.VMEM((tt + CONV_HALO, d_d), F32)],
        compiler_params=_params("parallel", "arbitrary"),
        name="conv_module",
    )(zd, conv_w, conv_b.reshape(1, d_d), ln_g.reshape(1, d_d), ln_b.reshape(1, d_d))


def _slot_columns(w, per_head, take_lo, take_n):
    rows = w.shape[0]
    wh = w.reshape(rows, C_HEADS, per_head)[:, :, take_lo:take_lo + take_n]
    wh = jnp.pad(wh, ((0, 0), (0, 0), (0, HEAD_SLOT - take_n)))
    return wh.reshape(rows, C_HEADS * HEAD_SLOT)


def _odd_in_weight(w):
    d = w.shape[0]
    lo = Q_LORA + KV_LORA
    kr = jnp.concatenate([jnp.zeros((d, ROPE_LO), w.dtype), w[:, lo:lo + QK_ROPE],
                          jnp.zeros((d, HEAD_SLOT - ROPE_LO - QK_ROPE), w.dtype)], axis=1)
    return jnp.concatenate([w[:, :lo], kr, w[:, lo + QK_ROPE:]], axis=1)


def kernel(x, norm_ffn1, ffn1_in, ffn1_out, norm_mix, norm_ffn2, ffn2_in, ffn2_out, even_w_in, even_w_out, gsu_ws, gsu_bs, gsu_ln_g, gsu_ln_b, shift_mu, decay_w0, decay_up, iclr_a0, iclr_up, gate_up, k_k, k_a, r_k, lnx_g, lnx_b, odd_w_in, odd_w_out, q_norm, wq_up, kv_norm, wkv_up, conv_w, conv_b, conv_ln_g, conv_ln_b, final_norm):
    batch, seq, d_model = x.shape
    depth = norm_ffn1.shape[0]
    d_a = gsu_ln_g.shape[1]
    d_b = decay_w0.shape[1]
    p_b = shift_mu.shape[1]
    d_d = conv_b.shape[1]
    h = x.reshape(batch * seq, d_model)
    cos_t, sin_t = _rope_tables(seq)
    for layer in range(depth):
        h = _ffn(h, norm_ffn1[layer], ffn1_in[layer].astype(BF16), ffn1_out[layer].astype(BF16))
        if layer % 2 == 0:
            e = layer // 2
            za, zb = _inproj(h, norm_mix[layer], even_w_in[e].astype(BF16), (2 * d_a, p_b), name="inproj_even")
            ya = _gsu(za, gsu_ws[e], gsu_bs[e], gsu_ln_g[e], gsu_ln_b[e])
            prep = _rwkv_prep(zb, batch, shift_mu[e], decay_w0[e], decay_up[e], iclr_a0[e], iclr_up[e],
                              gate_up[e], k_k[e], k_a[e], r_k[e])
            yb = _rwkv_scan(*prep, lnx_g[e], lnx_b[e], batch)
            h = _outproj(h, ya, yb, even_w_out[e].astype(BF16))
        else:
            o = layer // 2
            cq, ckv, kr, zd = _inproj(h, norm_mix[layer], _odd_in_weight(odd_w_in[o]).astype(BF16),
                                      (Q_LORA, KV_LORA, HEAD_SLOT, 2 * d_d), name="inproj_odd")
            wq = jnp.concatenate(
                [_slot_columns(wq_up[o], QK_NOPE + QK_ROPE, 0, QK_NOPE + QK_ROPE)], axis=1).astype(BF16)
            wk = _slot_columns(wkv_up[o], QK_NOPE + V_HEAD, 0, QK_NOPE).astype(BF16)
            wv = wkv_up[o].reshape(KV_LORA, C_HEADS, QK_NOPE + V_HEAD)[:, :, QK_NOPE:].reshape(
                KV_LORA, C_HEADS * V_HEAD).astype(BF16)
            q, k, v = _mla_prep(cq, ckv, kr, cos_t, sin_t, q_norm[o], kv_norm[o], wq, wk, wv, batch)
            yc = _attention(q, k, v, batch)
            yd = _conv(zd, conv_w[o], conv_b[o], conv_ln_g[o], conv_ln_b[o], batch)
            h = _outproj(h, yc, yd, odd_w_out[o].astype(BF16))
        last = layer == depth - 1
        h = _ffn(h, norm_ffn2[layer], ffn2_in[layer].astype(BF16), ffn2_out[layer].astype(BF16),
                 fin_g=final_norm if last else None)
    return h.reshape(batch, seq, d_model)
```

```python
import functools
import math

import jax
import jax.numpy as jnp
from jax import lax
from jax.experimental import pallas as pl
from jax.experimental.pallas import tpu as pltpu

F32 = jnp.float32
BF16 = jnp.bfloat16

LANES = 128
SUBLANES = 8
VMEM_LIMIT_BYTES = 56 * 1024 * 1024

CHUNK = 64
RMS_EPS = 1e-6
LN_EPS = 1e-5
LNX_EPS = 64e-5
GSU_BLOCK = 128
A_GROUPS = 4
B_HEAD = 64
LORA_W = 64
LORA_A = 64
LORA_G = 128
C_HEADS = 8
Q_LORA = 256
KV_LORA = 128
QK_NOPE = 64
QK_ROPE = 32
V_HEAD = 64
ROPE_THETA = 10000.0
ATTN_SCALE = (QK_NOPE + QK_ROPE) ** -0.5
CONV_K = 31
HEAD_SLOT = 128
ROPE_LO = QK_NOPE
ROPE_HALF = QK_ROPE // 2


def _params(*sem):
    return pltpu.CompilerParams(dimension_semantics=sem, vmem_limit_bytes=VMEM_LIMIT_BYTES)


def _dot(a, b):
    return jnp.dot(a, b, preferred_element_type=F32)


def _dot_nt(a, b):
    return lax.dot_general(a, b, (((1,), (1,)), ((), ())), preferred_element_type=F32)


def _dot_tn(a, b):
    return lax.dot_general(a, b, (((0,), (0,)), ((), ())), preferred_element_type=F32)


def _split_hi_lo(x):
    hi = x.astype(BF16)
    lo = (x - hi.astype(F32)).astype(BF16)
    return hi, lo


def _dot_exact_rhs(x, m):
    hi, lo = _split_hi_lo(x)
    return _dot(hi, m) + _dot(lo, m)


def _dot_exact_lhs(m, x):
    hi, lo = _split_hi_lo(x)
    return _dot(m, hi) + _dot(m, lo)


def _rms(x, g):
    ms = jnp.mean(x * x, axis=-1, keepdims=True)
    return x * lax.rsqrt(ms + RMS_EPS) * g


def _layernorm(x, g, b, eps):
    mu = jnp.mean(x, axis=-1, keepdims=True)
    xc = x - mu
    var = jnp.mean(xc * xc, axis=-1, keepdims=True)
    return xc * lax.rsqrt(var + eps) * g + b


def _sigmoid(x):
    return 1.0 / (1.0 + jnp.exp(-x))


def _full(shape):
    n = len(shape)
    return pl.BlockSpec(shape, lambda *_: (0,) * n)


FFN_CHUNK = 256


def _ffn_kernel(*refs, d_ff, final):
    if final:
        x_ref, g_ref, win_ref, wout_ref, fin_ref, o_ref, acc_ref = refs
    else:
        x_ref, g_ref, win_ref, wout_ref, o_ref, acc_ref = refs
    x = x_ref[...]
    h = _rms(x, g_ref[...]).astype(BF16)
    for j in range(d_ff // FFN_CHUNK):
        lo = j * FFN_CHUNK
        gate = _dot(h, win_ref[:, lo:lo + FFN_CHUNK])
        up = _dot(h, win_ref[:, d_ff + lo:d_ff + lo + FFN_CHUNK])
        act = (gate * _sigmoid(gate) * up).astype(BF16)
        part = _dot(act, wout_ref[lo:lo + FFN_CHUNK, :])
        if j == 0:
            acc_ref[...] = part
        else:
            acc_ref[...] += part
    y = x + 0.5 * acc_ref[...]
    if final:
        y = _rms(y, fin_ref[...])
    o_ref[...] = y


def _ffn(x, g, w_in, w_out, fin_g=None, tm=512):
    t, d = x.shape
    d_ff = w_out.shape[0]
    final = fin_g is not None
    in_specs = [pl.BlockSpec((tm, d), lambda i: (i, 0)), _full((1, d)),
                _full(w_in.shape), _full(w_out.shape)]
    args = [x, g.reshape(1, d), w_in, w_out]
    if final:
        in_specs.append(_full((1, d)))
        args.append(fin_g.reshape(1, d))
    return pl.pallas_call(
        functools.partial(_ffn_kernel, d_ff=d_ff, final=final),
        grid=(t // tm,),
        in_specs=in_specs,
        out_specs=pl.BlockSpec((tm, d), lambda i: (i, 0)),
        out_shape=jax.ShapeDtypeStruct((t, d), F32),
        scratch_shapes=[pltpu.VMEM((tm, d), F32)],
        compiler_params=_params("parallel"),
        name="ffn_final" if final else "ffn",
    )(*args)


def _inproj_kernel(x_ref, g_ref, w_ref, *o_refs, splits):
    h = _rms(x_ref[...], g_ref[...]).astype(BF16)
    lo = 0
    for o_ref, width in zip(o_refs, splits):
        o_ref[...] = _dot(h, w_ref[:, lo:lo + width])
        lo += width


def _inproj(x, g, w, splits, tm=512, name="inproj"):
    t, d = x.shape
    return pl.pallas_call(
        functools.partial(_inproj_kernel, splits=splits),
        grid=(t // tm,),
        in_specs=[pl.BlockSpec((tm, d), lambda i: (i, 0)), _full((1, d)), _full(w.shape)],
        out_specs=[pl.BlockSpec((tm, s), lambda i: (i, 0)) for s in splits],
        out_shape=[jax.ShapeDtypeStruct((t, s), F32) for s in splits],
        compiler_params=_params("parallel"),
        name=name,
    )(x, g.reshape(1, d), w)


def _outproj_kernel(x_ref, ya_ref, yb_ref, w_ref, o_ref):
    da = ya_ref.shape[1]
    acc = _dot(ya_ref[...].astype(BF16), w_ref[:da, :])
    acc += _dot(yb_ref[...].astype(BF16), w_ref[da:, :])
    o_ref[...] = x_ref[...] + acc


def _outproj(x, ya, yb, w, tm=512):
    t, d = x.shape
    return pl.pallas_call(
        _outproj_kernel,
        grid=(t // tm,),
        in_specs=[pl.BlockSpec((tm, d), lambda i: (i, 0)),
                  pl.BlockSpec((tm, ya.shape[1]), lambda i: (i, 0)),
                  pl.BlockSpec((tm, yb.shape[1]), lambda i: (i, 0)),
                  _full(w.shape)],
        out_specs=pl.BlockSpec((tm, d), lambda i: (i, 0)),
        out_shape=jax.ShapeDtypeStruct((t, d), F32),
        compiler_params=_params("parallel"),
        name="outproj",
    )(x, ya, yb, w)


def _gelu_tanh(x):
    c = math.sqrt(2.0 / math.pi)
    return 0.5 * x * (1.0 + jnp.tanh(c * (x + 0.044715 * (x * x * x))))


def _gsu_kernel(za_ref, ws_ref, bias_ref, lng_ref, lnb_ref, o_ref):
    d_a = o_ref.shape[1]
    gd = d_a // A_GROUPS
    g = _gelu_tanh(za_ref[...])
    u = g[:, :d_a]
    v = _layernorm(g[:, d_a:], lng_ref[...], lnb_ref[...], LN_EPS).astype(BF16)
    row = lax.broadcasted_iota(jnp.int32, (GSU_BLOCK, GSU_BLOCK), 0) // CHUNK
    col = lax.broadcasted_iota(jnp.int32, (GSU_BLOCK, GSU_BLOCK), 1) // CHUNK
    keep = row >= col
    for gi in range(A_GROUPS):
        w = jnp.where(keep, ws_ref[gi], 0.0).astype(BF16)
        for blk in range(za_ref.shape[0] // GSU_BLOCK):
            rs = slice(blk * GSU_BLOCK, (blk + 1) * GSU_BLOCK)
            cs = slice(gi * gd, (gi + 1) * gd)
            mixed = _dot(w, v[rs, cs]) + bias_ref[gi]
            o_ref[rs, cs] = u[rs, cs] * mixed


def _gsu(za, ws, bs, ln_g, ln_b, tb=512):
    t, two_da = za.shape
    d_a = two_da // 2
    bias = jnp.broadcast_to(bs[:, :, None], (A_GROUPS, GSU_BLOCK, d_a // A_GROUPS))
    return pl.pallas_call(
        _gsu_kernel,
        grid=(t // tb,),
        in_specs=[pl.BlockSpec((tb, two_da), lambda i: (i, 0)), _full(ws.shape), _full(bias.shape),
                  _full((1, d_a)), _full((1, d_a))],
        out_specs=pl.BlockSpec((tb, d_a), lambda i: (i, 0)),
        out_shape=jax.ShapeDtypeStruct((t, d_a), F32),
        compiler_params=_params("parallel"),
        name="gsu",
    )(za, ws, bias, ln_g.reshape(1, d_a), ln_b.reshape(1, d_a))


def _rwkv_prep_kernel(zb_ref, mu_ref, w0_ref, a0_ref, kk_ref, ka_ref, rk_ref, wup_ref, aup_ref, gup_ref,
                      ones_ref, r_o, k_o, v_o, ew_o, a_o, b_o, g_o, bonus_o, buf_ref, *, d_b):
    tt = zb_ref.shape[0]
    i = pl.program_id(1)

    @pl.when(i == 0)
    def _():
        buf_ref[0:SUBLANES, :] = jnp.zeros((SUBLANES, zb_ref.shape[1]), F32)

    zb = zb_ref[...]
    buf_ref[SUBLANES:SUBLANES + tt, :] = zb
    prev = buf_ref[SUBLANES - 1:SUBLANES - 1 + tt, :]
    z = zb + mu_ref[...] * (prev - zb)
    buf_ref[0:SUBLANES, :] = zb[tt - SUBLANES:, :]

    r = z[:, 0:d_b]
    k = z[:, d_b:2 * d_b]
    v = z[:, 2 * d_b:3 * d_b]
    xwa = z[:, 3 * d_b:3 * d_b + LORA_W + LORA_A]
    xg = z[:, 3 * d_b + LORA_W + LORA_A:]
    lane = lax.broadcasted_iota(jnp.int32, xwa.shape, 1)
    xwa = jnp.where(lane < LORA_W, jnp.tanh(xwa), xwa).astype(BF16)
    pre = w0_ref[...] + _dot(xwa, wup_ref[...])
    ew = _sigmoid(pre) * math.exp(-0.5)
    a = _sigmoid(a0_ref[...] + _dot(xwa, aup_ref[...]))
    g = _dot(_sigmoid(xg).astype(BF16), gup_ref[...])
    kk = k * kk_ref[...]
    ss = _dot_exact_rhs(kk * kk, ones_ref[...])
    kk = kk / jnp.maximum(jnp.sqrt(ss), 1e-12)
    k2 = k * (1.0 + (a - 1.0) * ka_ref[...])
    bonus = _dot_exact_rhs(r * k2 * rk_ref[...], ones_ref[...]) * v
    r_o[...] = r
    k_o[...] = k2
    v_o[...] = v
    ew_o[...] = ew
    a_o[...] = -kk
    b_o[...] = kk * a
    g_o[...] = g
    bonus_o[...] = bonus


def _head_ones(d, head):
    idx = jnp.arange(d) // head
    return (idx[:, None] == idx[None, :]).astype(BF16)


def _rwkv_prep(zb, batch, shift_mu, decay_w0, decay_up, iclr_a0, iclr_up, gate_up, k_k, k_a, r_k, tt=256):
    t, p_b = zb.shape
    d_b = decay_w0.shape[0]
    s = t // batch
    nt = s // tt
    zeros = jnp.zeros((LORA_W, d_b), F32)
    wup = jnp.concatenate([decay_up, zeros], axis=0).astype(BF16)
    aup = jnp.concatenate([zeros, iclr_up], axis=0).astype(BF16)
    row = lambda a: a.reshape(1, -1)
    tile = pl.BlockSpec((tt, d_b), lambda b, i: (b * nt + i, 0))
    return pl.pallas_call(
        functools.partial(_rwkv_prep_kernel, d_b=d_b),
        grid=(batch, nt),
        in_specs=[pl.BlockSpec((tt, p_b), lambda b, i: (b * nt + i, 0)), _full((1, p_b)),
                  _full((1, d_b)), _full((1, d_b)), _full((1, d_b)), _full((1, d_b)), _full((1, d_b)),
                  _full(wup.shape), _full(aup.shape), _full((LORA_G, d_b)), _full((d_b, d_b))],
        out_specs=[tile] * 8,
        out_shape=[jax.ShapeDtypeStruct((t, d_b), F32)] * 8,
        scratch_shapes=[pltpu.VMEM((tt + SUBLANES, p_b), F32)],
        compiler_params=_params("parallel", "arbitrary"),
        name="rwkv_prep",
    )(zb, row(shift_mu), row(decay_w0), row(iclr_a0), row(k_k), row(k_a), row(r_k),
      wup, aup, gate_up.astype(BF16), _head_ones(d_b, B_HEAD))


def _rwkv_scan_kernel(r_ref, k_ref, v_ref, ew_ref, a_ref, b_ref, g_ref, bonus_ref, lng_ref, lnb_ref,
                      ones_ref, o_ref, h_ref, y_ref):
    c, d_b = r_ref.shape
    heads = d_b // B_HEAD

    @pl.when(pl.program_id(1) == 0)
    def _():
        h_ref[...] = jnp.zeros(h_ref.shape, F32)

    ti = lax.broadcasted_iota(jnp.int32, (c, c), 0)
    si = lax.broadcasted_iota(jnp.int32, (c, c), 1)
    tril = (si <= ti).astype(BF16)
    strict = si < ti
    incl = si <= ti
    eye = (ti == si).astype(F32)

    ew = ew_ref[...]
    cum = _dot_exact_lhs(tril, ew)
    total = cum[c - 1:c, :]
    dec_in = jnp.exp(-cum)
    dec_ex = jnp.exp(ew - cum)
    grow = jnp.exp(cum)
    tail = jnp.exp(cum - total)
    a_t = a_ref[...] * dec_ex
    r_t = r_ref[...] * dec_in
    b_t = b_ref[...] * grow
    k_t = k_ref[...] * grow
    b_h = b_ref[...] * tail
    k_h = k_ref[...] * tail
    g_end = jnp.exp(-total)
    v = v_ref[...]

    hr = range(heads)
    sl = [slice(h * B_HEAD, (h + 1) * B_HEAD) for h in hr]
    bf = lambda t: t.astype(BF16)
    vh = [bf(v[:, s]) for s in sl]
    p = [_dot_nt(bf(jnp.concatenate([a_t[:, s], r_t[:, s]], axis=0)),
                 bf(jnp.concatenate([b_t[:, s], k_t[:, s]], axis=0))) for s in sl]
    nk = [jnp.where(strict, p[h][:c, :c], 0.0) for h in hr]
    a_ak = [bf(jnp.where(strict, p[h][:c, c:], 0.0)) for h in hr]
    m_rb = [bf(jnp.where(incl, p[h][c:, :c], 0.0)) for h in hr]
    m_rk = [bf(jnp.where(incl, p[h][c:, c:], 0.0)) for h in hr]
    x = [jnp.concatenate([a_t[:, sl[h]], _dot(a_ak[h], vh[h])], axis=1) for h in hr]
    y_loc = [_dot(m_rk[h], vh[h]) for h in hr]
    g_loc = [_dot_tn(bf(k_h[:, sl[h]]), vh[h]) for h in hr]
    steps = int(math.log2(c))
    for it in range(steps):
        nkb = [bf(nk[h]) for h in hr]
        x = [x[h] + _dot(nkb[h], bf(x[h])) for h in hr]
        if it + 1 < steps:
            nk = [_dot(nkb[h], nkb[h]) for h in hr]
    xb = [bf(x[h]) for h in hr]
    ry = [_dot(m_rb[h], xb[h]) + jnp.concatenate([r_t[:, sl[h]], y_loc[h]], axis=1) for h in hr]
    pg = [_dot_tn(bf(b_h[:, sl[h]]), xb[h])
          + jnp.concatenate([eye * g_end[:, sl[h]], g_loc[h]], axis=1) for h in hr]
    h0b = [bf(h_ref[h]) for h in hr]
    ys = [_dot(bf(ry[h][:, :B_HEAD]), h0b[h]) + ry[h][:, B_HEAD:] for h in hr]
    h_new = [_dot(bf(pg[h][:, :B_HEAD]), h0b[h]) + pg[h][:, B_HEAD:] for h in hr]

    for h in range(heads):
        h_ref[h] = h_new[h]
    y = jnp.concatenate(ys, axis=1)
    inv = 1.0 / B_HEAD
    mu = _dot_exact_rhs(y, ones_ref[...]) * inv
    yc = y - mu
    var = _dot_exact_rhs(yc * yc, ones_ref[...]) * inv
    yn = yc * lax.rsqrt(var + LNX_EPS) * lng_ref[...] + lnb_ref[...]
    o_ref[...] = (yn + bonus_ref[...]) * g_ref[...]


def _rwkv_scan(r, k, v, ew, a, b, g, bonus, lnx_g, lnx_b, batch):
    t, d_b = r.shape
    s = t // batch
    nc = s // CHUNK
    tile = pl.BlockSpec((CHUNK, d_b), lambda bi, i: (bi * nc + i, 0))
    return pl.pallas_call(
        _rwkv_scan_kernel,
        grid=(batch, nc),
        in_specs=[tile] * 8 + [_full((1, d_b)), _full((1, d_b)), _full((d_b, d_b))],
        out_specs=tile,
        out_shape=jax.ShapeDtypeStruct((t, d_b), F32),
        scratch_shapes=[pltpu.VMEM((d_b // B_HEAD, B_HEAD, B_HEAD), F32), pltpu.VMEM((CHUNK, d_b), F32)],
        compiler_params=_params("parallel", "arbitrary"),
        name="rwkv_scan",
    )(r, k, v, ew, a, b, g, bonus, lnx_g.reshape(1, d_b), lnx_b.reshape(1, d_b), _head_ones(d_b, B_HEAD))


def _rope_table_kernel(cos_ref, sin_ref):
    rows = cos_ref.shape[0]
    pos = (pl.program_id(0) * rows + lax.broadcasted_iota(jnp.int32, cos_ref.shape, 0)).astype(F32)
    lane = lax.broadcasted_iota(jnp.int32, cos_ref.shape, 1)
    in_rope = (lane >= ROPE_LO) & (lane < ROPE_LO + QK_ROPE)
    second = lane >= ROPE_LO + ROPE_HALF
    idx = jnp.where(second, lane - ROPE_LO - ROPE_HALF, lane - ROPE_LO).astype(F32)
    inv_freq = jnp.exp(idx * (-2.0 * math.log(ROPE_THETA) / QK_ROPE))
    ang = pos * inv_freq
    cos_ref[...] = jnp.where(in_rope, jnp.cos(ang), 1.0)
    sin_ref[...] = jnp.where(in_rope, jnp.where(second, jnp.sin(ang), -jnp.sin(ang)), 0.0)


def _rope_tables(s, rows=256):
    spec = pl.BlockSpec((rows, HEAD_SLOT), lambda i: (i, 0))
    return pl.pallas_call(
        _rope_table_kernel,
        grid=(s // rows,),
        out_specs=[spec, spec],
        out_shape=[jax.ShapeDtypeStruct((s, HEAD_SLOT), F32)] * 2,
        compiler_params=_params("parallel"),
        name="rope_tables",
    )()


def _rope_slot(x, cos_t, sin_t):
    lane = lax.broadcasted_iota(jnp.int32, x.shape, 1)
    partner = jnp.where(lane < ROPE_LO + ROPE_HALF,
                        pltpu.roll(x, HEAD_SLOT - ROPE_HALF, 1), pltpu.roll(x, ROPE_HALF, 1))
    return x * cos_t + partner * sin_t


def _mla_prep_kernel(cq_ref, ckv_ref, kr_ref, cos_ref, sin_ref, qn_ref, kvn_ref, wq_ref, wk_ref, wv_ref,
                     q_o, k_o, v_o):
    cos_t = cos_ref[...]
    sin_t = sin_ref[...]
    cqn = _rms(cq_ref[...], qn_ref[...]).astype(BF16)
    ckvn = _rms(ckv_ref[...], kvn_ref[...]).astype(BF16)
    q = _dot(cqn, wq_ref[...])
    kn = _dot(ckvn, wk_ref[...])
    v_o[...] = _dot(ckvn, wv_ref[...]).astype(BF16)
    krot = _rope_slot(kr_ref[...], cos_t, sin_t)
    for h in range(C_HEADS):
        hs = slice(h * HEAD_SLOT, (h + 1) * HEAD_SLOT)
        q_o[0, h] = (_rope_slot(q[:, hs], cos_t, sin_t) * ATTN_SCALE).astype(BF16)
        k_o[0, h] = (kn[:, hs] + krot).astype(BF16)


def _mla_prep(cq, ckv, kr, cos_t, sin_t, q_norm, kv_norm, wq, wk, wv, batch, tm=256):
    t = cq.shape[0]
    s = t // batch
    nt = s // tm
    rows = lambda w: pl.BlockSpec((tm, w), lambda b, i: (b * nt + i, 0))
    tab = pl.BlockSpec((tm, HEAD_SLOT), lambda b, i: (i, 0))
    qk_spec = pl.BlockSpec((1, C_HEADS, tm, HEAD_SLOT), lambda b, i: (b, 0, i, 0))
    qk_shape = jax.ShapeDtypeStruct((batch, C_HEADS, s, HEAD_SLOT), BF16)
    return pl.pallas_call(
        _mla_prep_kernel,
        grid=(batch, nt),
        in_specs=[rows(Q_LORA), rows(KV_LORA), rows(HEAD_SLOT), tab, tab,
                  _full((1, Q_LORA)), _full((1, KV_LORA)), _full(wq.shape), _full(wk.shape), _full(wv.shape)],
        out_specs=[qk_spec, qk_spec, rows(C_HEADS * V_HEAD)],
        out_shape=[qk_shape, qk_shape, jax.ShapeDtypeStruct((t, C_HEADS * V_HEAD), BF16)],
        compiler_params=_params("parallel", "parallel"),
        name="mla_prep",
    )(cq, ckv, kr, cos_t, sin_t, q_norm.reshape(1, -1), kv_norm.reshape(1, -1), wq, wk, wv)


ATTN_TQ = 512


def _attn_kernel(q_ref, k_ref, v_ref, o_ref):
    tq = q_ref.shape[2]
    qi = pl.program_id(2)
    lane = lax.broadcasted_iota(jnp.int32, (tq, 2 * V_HEAD), 1)
    row_c = lax.broadcasted_iota(jnp.int32, (tq, tq), 0) // CHUNK
    col_c = lax.broadcasted_iota(jnp.int32, (tq, tq), 1) // CHUNK
    diag_keep = col_c <= row_c
    pair = range(2)
    q = [q_ref[0, hh] for hh in pair]

    def block(j, carry, masked):
        m, l, acc = carry
        start = pl.multiple_of(j * tq, tq)
        vb = v_ref[pl.ds(start, tq), :]
        sc = [_dot_nt(q[hh], k_ref[0, hh, pl.ds(start, tq), :]) for hh in pair]
        if masked:
            sc = [jnp.where(diag_keep, s, -jnp.inf) for s in sc]
        m_new = [jnp.maximum(m[hh], jnp.max(sc[hh], axis=-1, keepdims=True)) for hh in pair]
        alpha = [jnp.exp(m[hh] - m_new[hh]) for hh in pair]
        p = [jnp.exp(sc[hh] - m_new[hh]) for hh in pair]
        l = [alpha[hh] * l[hh] + jnp.sum(p[hh], axis=-1, keepdims=True) for hh in pair]
        pv = [_dot(p[hh].astype(BF16), vb) for hh in pair]
        acc = [alpha[hh] * acc[hh] + pv[hh] for hh in pair]
        return m_new, l, acc

    init = ([jnp.full((tq, 1), -jnp.inf, F32)] * 2, [jnp.zeros((tq, 1), F32)] * 2,
            [jnp.zeros((tq, 2 * V_HEAD), F32)] * 2)
    carry = lax.fori_loop(0, qi, functools.partial(block, masked=False), init)
    m, l, acc = block(qi, carry, True)
    o_ref[...] = jnp.where(lane < V_HEAD, acc[0] / l[0], acc[1] / l[1])


def _attention(q, k, v, batch):
    _, heads, s, slot = q.shape
    nq = s // ATTN_TQ
    t = batch * s
    return pl.pallas_call(
        _attn_kernel,
        grid=(batch, heads // 2, nq),
        in_specs=[pl.BlockSpec((1, 2, ATTN_TQ, slot), lambda b, p, i: (b, p, i, 0)),
                  pl.BlockSpec((1, 2, s, slot), lambda b, p, i: (b, p, 0, 0)),
                  pl.BlockSpec((s, 2 * V_HEAD), lambda b, p, i: (b, p))],
        out_specs=pl.BlockSpec((ATTN_TQ, 2 * V_HEAD), lambda b, p, i: (b * nq + i, p)),
        out_shape=jax.ShapeDtypeStruct((t, heads * V_HEAD), F32),
        compiler_params=_params("parallel", "parallel", "arbitrary"),
        name="mla_attention",
    )(q, k, v)


CONV_HALO = 32


def _conv_kernel(zd_ref, w_ref, b_ref, lng_ref, lnb_ref, o_ref, buf_ref):
    tt, d_d = o_ref.shape

    @pl.when(pl.program_id(1) == 0)
    def _():
        buf_ref[0:CONV_HALO, :] = jnp.zeros((CONV_HALO, d_d), F32)

    zd = zd_ref[...]
    buf_ref[CONV_HALO:CONV_HALO + tt, :] = zd[:, :d_d] * _sigmoid(zd[:, d_d:])
    acc = jnp.zeros((tt, d_d), F32) + b_ref[...]
    first = CONV_HALO - (CONV_K - 1)
    for j in range(CONV_K):
        acc = acc + w_ref[j:j + 1, :] * buf_ref[first + j:first + j + tt, :]
    buf_ref[0:CONV_HALO, :] = buf_ref[tt:tt + CONV_HALO, :]
    y = _layernorm(acc, lng_ref[...], lnb_ref[...], LN_EPS)
    o_ref[...] = y * _sigmoid(y)


def _conv(zd, conv_w, conv_b, ln_g, ln_b, batch, tt=256):
    t, two_d = zd.shape
    d_d = two_d // 2
    nt = (t // batch) // tt
    return pl.pallas_call(
        _conv_kernel,
        grid=(batch, nt),
        in_specs=[pl.BlockSpec((tt, two_d), lambda b, i: (b * nt + i, 0)), _full(conv_w.shape),
                  _full((1, d_d)), _full((1, d_d)), _full((1, d_d))],
        out_specs=pl.BlockSpec((tt, d_d), lambda b, i: (b * nt + i, 0)),
        out_shape=jax.ShapeDtypeStruct((t, d_d), F32),
        scratch_shapes=[pltpu.VMEM((tt + CONV_HALO, d_d), F32)],
        compiler_params=_params("parallel", "arbitrary"),
        name="conv_module",
    )(zd, conv_w, conv_b.reshape(1, d_d), ln_g.reshape(1, d_d), ln_b.reshape(1, d_d))


def _slot_columns(w, per_head, take_lo, take_n):
    rows = w.shape[0]
    wh = w.reshape(rows, C_HEADS, per_head)[:, :, take_lo:take_lo + take_n]
    wh = jnp.pad(wh, ((0, 0), (0, 0), (0, HEAD_SLOT - take_n)))
    return wh.reshape(rows, C_HEADS * HEAD_SLOT)


def _odd_in_weight(w):
    d = w.shape[0]
    lo = Q_LORA + KV_LORA
    kr = jnp.concatenate([jnp.zeros((d, ROPE_LO), w.dtype), w[:, lo:lo + QK_ROPE],
                          jnp.zeros((d, HEAD_SLOT - ROPE_LO - QK_ROPE), w.dtype)], axis=1)
    return jnp.concatenate([w[:, :lo], kr, w[:, lo + QK_ROPE:]], axis=1)


def kernel(x, norm_ffn1, ffn1_in, ffn1_out, norm_mix, norm_ffn2, ffn2_in, ffn2_out, even_w_in, even_w_out, gsu_ws, gsu_bs, gsu_ln_g, gsu_ln_b, shift_mu, decay_w0, decay_up, iclr_a0, iclr_up, gate_up, k_k, k_a, r_k, lnx_g, lnx_b, odd_w_in, odd_w_out, q_norm, wq_up, kv_norm, wkv_up, conv_w, conv_b, conv_ln_g, conv_ln_b, final_norm):
    batch, seq, d_model = x.shape
    depth = norm_ffn1.shape[0]
    d_a = gsu_ln_g.shape[1]
    d_b = decay_w0.shape[1]
    p_b = shift_mu.shape[1]
    d_d = conv_b.shape[1]
    h = x.reshape(batch * seq, d_model)
    cos_t, sin_t = _rope_tables(seq)
    for layer in range(depth):
        h = _ffn(h, norm_ffn1[layer], ffn1_in[layer].astype(BF16), ffn1_out[layer].astype(BF16))
        if layer % 2 == 0:
            e = layer // 2
            za, zb = _inproj(h, norm_mix[layer], even_w_in[e].astype(BF16), (2 * d_a, p_b), name="inproj_even")
            ya = _gsu(za, gsu_ws[e], gsu_bs[e], gsu_ln_g[e], gsu_ln_b[e])
            prep = _rwkv_prep(zb, batch, shift_mu[e], decay_w0[e], decay_up[e], iclr_a0[e], iclr_up[e],
                              gate_up[e], k_k[e], k_a[e], r_k[e])
            yb = _rwkv_scan(*prep, lnx_g[e], lnx_b[e], batch)
            h = _outproj(h, ya, yb, even_w_out[e].astype(BF16))
        else:
            o = layer // 2
            cq, ckv, kr, zd = _inproj(h, norm_mix[layer], _odd_in_weight(odd_w_in[o]).astype(BF16),
                                      (Q_LORA, KV_LORA, HEAD_SLOT, 2 * d_d), name="inproj_odd")
            wq = jnp.concatenate(
                [_slot_columns(wq_up[o], QK_NOPE + QK_ROPE, 0, QK_NOPE + QK_ROPE)], axis=1).astype(BF16)
            wk = _slot_columns(wkv_up[o], QK_NOPE + V_HEAD, 0, QK_NOPE).astype(BF16)
            wv = wkv_up[o].reshape(KV_LORA, C_HEADS, QK_NOPE + V_HEAD)[:, :, QK_NOPE:].reshape(
                KV_LORA, C_HEADS * V_HEAD).astype(BF16)
            q, k, v = _mla_prep(cq, ckv, kr, cos_t, sin_t, q_norm[o], kv_norm[o], wq, wk, wv, batch)
            yc = _attention(q, k, v, batch)
            yd = _conv(zd, conv_w[o], conv_b[o], conv_ln_g[o], conv_ln_b[o], batch)
            h = _outproj(h, yc, yd, odd_w_out[o].astype(BF16))
        last = layer == depth - 1
        h = _ffn(h, norm_ffn2[layer], ffn2_in[layer].astype(BF16), ffn2_out[layer].astype(BF16),
                 fin_g=final_norm if last else None)
    return h.reshape(batch, seq, d_model)
```

```python
import functools
import math

import jax
import jax.numpy as jnp
from jax import lax
from jax.experimental import pallas as pl
from jax.experimental.pallas import tpu as pltpu

F32 = jnp.float32
BF16 = jnp.bfloat16

LANES = 128
SUBLANES = 8
VMEM_LIMIT_BYTES = 56 * 1024 * 1024

CHUNK = 64
RMS_EPS = 1e-6
LN_EPS = 1e-5
LNX_EPS = 64e-5
GSU_BLOCK = 128
A_GROUPS = 4
B_HEAD = 64
LORA_W = 64
LORA_A = 64
LORA_G = 128
C_HEADS = 8
Q_LORA = 256
KV_LORA = 128
QK_NOPE = 64
QK_ROPE = 32
V_HEAD = 64
ROPE_THETA = 10000.0
ATTN_SCALE = (QK_NOPE + QK_ROPE) ** -0.5
CONV_K = 31
HEAD_SLOT = 128
ROPE_LO = QK_NOPE
ROPE_HALF = QK_ROPE // 2


def _params(*sem):
    return pltpu.CompilerParams(dimension_semantics=sem, vmem_limit_bytes=VMEM_LIMIT_BYTES)


def _dot(a, b):
    return jnp.dot(a, b, preferred_element_type=F32)


def _dot_nt(a, b):
    return lax.dot_general(a, b, (((1,), (1,)), ((), ())), preferred_element_type=F32)


def _dot_tn(a, b):
    return lax.dot_general(a, b, (((0,), (0,)), ((), ())), preferred_element_type=F32)


def _split_hi_lo(x):
    hi = x.astype(BF16)
    lo = (x - hi.astype(F32)).astype(BF16)
    return hi, lo


def _dot_exact_rhs(x, m):
    hi, lo = _split_hi_lo(x)
    return _dot(hi, m) + _dot(lo, m)


def _dot_exact_lhs(m, x):
    hi, lo = _split_hi_lo(x)
    return _dot(m, hi) + _dot(m, lo)


def _rms(x, g):
    ms = jnp.mean(x * x, axis=-1, keepdims=True)
    return x * lax.rsqrt(ms + RMS_EPS) * g


def _layernorm(x, g, b, eps):
    mu = jnp.mean(x, axis=-1, keepdims=True)
    xc = x - mu
    var = jnp.mean(xc * xc, axis=-1, keepdims=True)
    return xc * lax.rsqrt(var + eps) * g + b


def _sigmoid(x):
    return 1.0 / (1.0 + jnp.exp(-x))


def _full(shape):
    n = len(shape)
    return pl.BlockSpec(shape, lambda *_: (0,) * n)


FFN_CHUNK = 256


def _ffn_kernel(*refs, d_ff, pre, splits, final):
    refs = list(refs)
    x_ref = refs.pop(0)
    if pre:
        ya_ref, yb_ref, wmo_ref = refs.pop(0), refs.pop(0), refs.pop(0)
    g_ref, win_ref, wout_ref = refs.pop(0), refs.pop(0), refs.pop(0)
    if splits:
        gm_ref, wmi_ref = refs.pop(0), refs.pop(0)
    if final:
        fin_ref = refs.pop(0)
    o_ref = refs.pop(0)
    z_refs = [refs.pop(0) for _ in splits]
    (acc_ref,) = refs

    x = x_ref[...]
    if pre:
        da = ya_ref.shape[1]
        x = x + _dot(ya_ref[...].astype(BF16), wmo_ref[:da, :]) + _dot(yb_ref[...].astype(BF16), wmo_ref[da:, :])
    h = _rms(x, g_ref[...]).astype(BF16)
    for j in range(d_ff // FFN_CHUNK):
        lo = j * FFN_CHUNK
        gate = _dot(h, win_ref[:, lo:lo + FFN_CHUNK])
        up = _dot(h, win_ref[:, d_ff + lo:d_ff + lo + FFN_CHUNK])
        act = (gate * _sigmoid(gate) * up).astype(BF16)
        part = _dot(act, wout_ref[lo:lo + FFN_CHUNK, :])
        if j == 0:
            acc_ref[...] = part
        else:
            acc_ref[...] += part
    y = x + 0.5 * acc_ref[...]
    if final:
        y = _rms(y, fin_ref[...])
    o_ref[...] = y
    if splits:
        h2 = _rms(y, gm_ref[...]).astype(BF16)
        lo = 0
        for z_ref, width in zip(z_refs, splits):
            z_ref[...] = _dot(h2, wmi_ref[:, lo:lo + width]).astype(z_ref.dtype)
            lo += width


def _ffn(x, g, w_in, w_out, pre=None, post=None, fin_g=None, tm=512, name="ffn"):
    t, d = x.shape
    d_ff = w_out.shape[0]
    rows = lambda w: pl.BlockSpec((tm, w), lambda i: (i, 0))
    in_specs, args = [rows(d)], [x]
    if pre is not None:
        ya, yb, wmo = pre
        in_specs += [rows(ya.shape[1]), rows(yb.shape[1]), _full(wmo.shape)]
        args += [ya, yb, wmo]
    in_specs += [_full((1, d)), _full(w_in.shape), _full(w_out.shape)]
    args += [g.reshape(1, d), w_in, w_out]
    splits = ()
    if post is not None:
        gm, wmi, splits = post
        in_specs += [_full((1, d)), _full(wmi.shape)]
        args += [gm.reshape(1, d), wmi]
    if fin_g is not None:
        in_specs.append(_full((1, d)))
        args.append(fin_g.reshape(1, d))
    out_specs = [rows(d)] + [rows(s) for s in splits]
    out_shape = [jax.ShapeDtypeStruct((t, d), F32)] + [jax.ShapeDtypeStruct((t, s), F32) for s in splits]
    return pl.pallas_call(
        functools.partial(_ffn_kernel, d_ff=d_ff, pre=pre is not None, splits=tuple(splits),
                          final=fin_g is not None),
        grid=(t // tm,),
        in_specs=in_specs,
        out_specs=out_specs,
        out_shape=out_shape,
        scratch_shapes=[pltpu.VMEM((tm, d), F32)],
        compiler_params=_params("parallel"),
        name=name,
    )(*args)


def _gelu_tanh(x):
    c = math.sqrt(2.0 / math.pi)
    return 0.5 * x * (1.0 + jnp.tanh(c * (x + 0.044715 * (x * x * x))))


def _gsu_kernel(za_ref, ws_ref, bias_ref, lng_ref, lnb_ref, o_ref):
    d_a = o_ref.shape[1]
    gd = d_a // A_GROUPS
    g = _gelu_tanh(za_ref[...])
    u = g[:, :d_a]
    v = _layernorm(g[:, d_a:], lng_ref[...], lnb_ref[...], LN_EPS).astype(BF16)
    row = lax.broadcasted_iota(jnp.int32, (GSU_BLOCK, GSU_BLOCK), 0) // CHUNK
    col = lax.broadcasted_iota(jnp.int32, (GSU_BLOCK, GSU_BLOCK), 1) // CHUNK
    keep = row >= col
    for gi in range(A_GROUPS):
        w = jnp.where(keep, ws_ref[gi], 0.0).astype(BF16)
        for blk in range(za_ref.shape[0] // GSU_BLOCK):
            rs = slice(blk * GSU_BLOCK, (blk + 1) * GSU_BLOCK)
            cs = slice(gi * gd, (gi + 1) * gd)
            mixed = _dot(w, v[rs, cs]) + bias_ref[gi]
            o_ref[rs, cs] = u[rs, cs] * mixed


def _gsu(za, ws, bs, ln_g, ln_b, tb=512):
    t, two_da = za.shape
    d_a = two_da // 2
    bias = jnp.broadcast_to(bs[:, :, None], (A_GROUPS, GSU_BLOCK, d_a // A_GROUPS))
    return pl.pallas_call(
        _gsu_kernel,
        grid=(t // tb,),
        in_specs=[pl.BlockSpec((tb, two_da), lambda i: (i, 0)), _full(ws.shape), _full(bias.shape),
                  _full((1, d_a)), _full((1, d_a))],
        out_specs=pl.BlockSpec((tb, d_a), lambda i: (i, 0)),
        out_shape=jax.ShapeDtypeStruct((t, d_a), F32),
        compiler_params=_params("parallel"),
        name="gsu",
    )(za, ws, bias, ln_g.reshape(1, d_a), ln_b.reshape(1, d_a))


RWKV_NB = 4


def _rwkv_kernel(zb_ref, mu_ref, w0_ref, a0_ref, kk_ref, ka_ref, rk_ref, wup_ref, aup_ref, gup_ref,
                 lng_ref, lnb_ref, ones_ref, o_ref, h_ref, buf_ref):
    nb, c, p_b = zb_ref.shape
    d_b = o_ref.shape[2]
    heads = d_b // B_HEAD

    @pl.when(pl.program_id(1) == 0)
    def _():
        h_ref[...] = jnp.zeros(h_ref.shape, F32)
        buf_ref[:, 0:SUBLANES, :] = jnp.zeros((nb, SUBLANES, p_b), F32)

    zbs, prevs = [], []
    for n in range(nb):
        zn = zb_ref[n]
        buf_ref[n, SUBLANES:SUBLANES + c, :] = zn
        prevs.append(buf_ref[n, SUBLANES - 1:SUBLANES - 1 + c, :])
        buf_ref[n, 0:SUBLANES, :] = zn[c - SUBLANES:, :]
        zbs.append(zn)
    zb = jnp.concatenate(zbs, axis=0)
    z = zb + mu_ref[...] * (jnp.concatenate(prevs, axis=0) - zb)

    r = z[:, 0:d_b]
    k = z[:, d_b:2 * d_b]
    v = z[:, 2 * d_b:3 * d_b]
    xwa = z[:, 3 * d_b:3 * d_b + LORA_W + LORA_A]
    xg = z[:, 3 * d_b + LORA_W + LORA_A:]
    lane = lax.broadcasted_iota(jnp.int32, xwa.shape, 1)
    xwa = jnp.where(lane < LORA_W, jnp.tanh(xwa), xwa).astype(BF16)
    ew = _sigmoid(w0_ref[...] + _dot(xwa, wup_ref[...])) * math.exp(-0.5)
    a_lr = _sigmoid(a0_ref[...] + _dot(xwa, aup_ref[...]))
    gate = _dot(_sigmoid(xg).astype(BF16), gup_ref[...])
    ones = ones_ref[...]
    kk = k * kk_ref[...]
    kk = kk / jnp.maximum(jnp.sqrt(_dot_exact_rhs(kk * kk, ones)), 1e-12)
    k = k * (1.0 + (a_lr - 1.0) * ka_ref[...])
    bonus = _dot((r * k * rk_ref[...]).astype(BF16), ones) * v
    a_vec = -kk
    b_vec = kk * a_lr

    ti = lax.broadcasted_iota(jnp.int32, (c, c), 0)
    si = lax.broadcasted_iota(jnp.int32, (c, c), 1)
    tril = (si <= ti).astype(BF16)
    strict = si < ti
    incl = si <= ti
    eye = (ti == si).astype(F32)

    cum = jnp.concatenate([_dot_exact_lhs(tril, ew[n * c:(n + 1) * c]) for n in range(nb)], axis=0)
    total = jnp.concatenate(
        [jnp.broadcast_to(cum[(n + 1) * c - 1:(n + 1) * c], (c, d_b)) for n in range(nb)], axis=0)
    grow = jnp.exp(cum)
    tail = jnp.exp(cum - total)
    a_t = a_vec * jnp.exp(ew - cum)
    r_t = r * jnp.exp(-cum)
    b_t = b_vec * grow
    k_t = k * grow
    b_h = b_vec * tail
    k_h = k * tail
    g_end = [jnp.exp(-cum[(n + 1) * c - 1:(n + 1) * c]) for n in range(nb)]

    bf = lambda t: t.astype(BF16)
    chains = [(n, h) for n in range(nb) for h in range(heads)]
    cr = range(len(chains))
    sel = [(slice(n * c, (n + 1) * c), slice(h * B_HEAD, (h + 1) * B_HEAD)) for n, h in chains]
    cut = lambda t, i: t[sel[i][0], sel[i][1]]
    vh = [bf(cut(v, i)) for i in cr]
    p = [_dot_nt(bf(jnp.concatenate([cut(a_t, i), cut(r_t, i)], axis=0)),
                 bf(jnp.concatenate([cut(b_t, i), cut(k_t, i)], axis=0))) for i in cr]
    nk = [jnp.where(strict, p[i][:c, :c], 0.0) for i in cr]
    a_ak = [bf(jnp.where(strict, p[i][:c, c:], 0.0)) for i in cr]
    m_rb = [bf(jnp.where(incl, p[i][c:, :c], 0.0)) for i in cr]
    m_rk = [bf(jnp.where(incl, p[i][c:, c:], 0.0)) for i in cr]
    x = [jnp.concatenate([cut(a_t, i), _dot(a_ak[i], vh[i])], axis=1) for i in cr]
    y_loc = [_dot(m_rk[i], vh[i]) for i in cr]
    g_loc = [_dot_tn(bf(cut(k_h, i)), vh[i]) for i in cr]
    steps = int(math.log2(c))
    for it in range(steps):
        nkb = [bf(nk[i]) for i in cr]
        x = [x[i] + _dot(nkb[i], bf(x[i])) for i in cr]
        if it + 1 < steps:
            nk = [_dot(nkb[i], nkb[i]) for i in cr]
    xb = [bf(x[i]) for i in cr]
    ry = [_dot(m_rb[i], xb[i]) + jnp.concatenate([cut(r_t, i), y_loc[i]], axis=1) for i in cr]
    pg = [_dot_tn(bf(cut(b_h, i)), xb[i])
          + jnp.concatenate([eye * g_end[chains[i][0]][:, sel[i][1]], g_loc[i]], axis=1) for i in cr]
    h0b = [bf(h_ref[i]) for i in cr]
    ys = [_dot(bf(ry[i][:, :B_HEAD]), h0b[i]) + ry[i][:, B_HEAD:] for i in cr]
    h_new = [_dot(bf(pg[i][:, :B_HEAD]), h0b[i]) + pg[i][:, B_HEAD:] for i in cr]
    for i in cr:
        h_ref[i] = h_new[i]

    y = jnp.concatenate(
        [jnp.concatenate(ys[n * heads:(n + 1) * heads], axis=1) for n in range(nb)], axis=0)
    inv = 1.0 / B_HEAD
    mu = _dot(bf(y), ones) * inv
    yc = y - mu
    var = _dot(bf(yc * yc), ones) * inv
    yn = yc * lax.rsqrt(var + LNX_EPS) * lng_ref[...] + lnb_ref[...]
    out = (yn + bonus) * gate
    for n in range(nb):
        o_ref[n] = out[n * c:(n + 1) * c]


def _head_ones(d, head):
    idx = jnp.arange(d) // head
    return (idx[:, None] == idx[None, :]).astype(BF16)


def _rwkv(zb, shift_mu, decay_w0, decay_up, iclr_a0, iclr_up, gate_up, k_k, k_a, r_k, lnx_g, lnx_b):
    batch, s, p_b = zb.shape
    d_b = decay_w0.shape[0]
    nb = RWKV_NB
    zeros = jnp.zeros((LORA_W, d_b), F32)
    wup = jnp.concatenate([decay_up, zeros], axis=0).astype(BF16)
    aup = jnp.concatenate([zeros, iclr_up], axis=0).astype(BF16)
    row = lambda a: a.reshape(1, -1)
    vec = _full((1, d_b))
    return pl.pallas_call(
        _rwkv_kernel,
        grid=(batch // nb, s // CHUNK),
        in_specs=[pl.BlockSpec((nb, CHUNK, p_b), lambda b, i: (b, i, 0)), _full((1, p_b)),
                  vec, vec, vec, vec, vec, _full(wup.shape), _full(aup.shape), _full((LORA_G, d_b)),
                  vec, vec, _full((d_b, d_b))],
        out_specs=pl.BlockSpec((nb, CHUNK, d_b), lambda b, i: (b, i, 0)),
        out_shape=jax.ShapeDtypeStruct((batch, s, d_b), F32),
        scratch_shapes=[pltpu.VMEM((nb * (d_b // B_HEAD), B_HEAD, B_HEAD), F32),
                        pltpu.VMEM((nb, CHUNK + SUBLANES, p_b), F32)],
        compiler_params=_params("parallel", "arbitrary"),
        name="rwkv",
    )(zb, row(shift_mu), row(decay_w0), row(iclr_a0), row(k_k), row(k_a), row(r_k),
      wup, aup, gate_up.astype(BF16), row(lnx_g), row(lnx_b), _head_ones(d_b, B_HEAD))


def _rope_table_kernel(cos_ref, sin_ref):
    rows = cos_ref.shape[0]
    pos = (pl.program_id(0) * rows + lax.broadcasted_iota(jnp.int32, cos_ref.shape, 0)).astype(F32)
    lane = lax.broadcasted_iota(jnp.int32, cos_ref.shape, 1)
    in_rope = (lane >= ROPE_LO) & (lane < ROPE_LO + QK_ROPE)
    second = lane >= ROPE_LO + ROPE_HALF
    idx = jnp.where(second, lane - ROPE_LO - ROPE_HALF, lane - ROPE_LO).astype(F32)
    inv_freq = jnp.exp(idx * (-2.0 * math.log(ROPE_THETA) / QK_ROPE))
    ang = pos * inv_freq
    cos_ref[...] = jnp.where(in_rope, jnp.cos(ang), 1.0)
    sin_ref[...] = jnp.where(in_rope, jnp.where(second, jnp.sin(ang), -jnp.sin(ang)), 0.0)


def _rope_tables(s, rows=256):
    spec = pl.BlockSpec((rows, HEAD_SLOT), lambda i: (i, 0))
    return pl.pallas_call(
        _rope_table_kernel,
        grid=(s // rows,),
        out_specs=[spec, spec],
        out_shape=[jax.ShapeDtypeStruct((s, HEAD_SLOT), F32)] * 2,
        compiler_params=_params("parallel"),
        name="rope_tables",
    )()


def _mla_prep_kernel(cq_ref, ckv_ref, kr_ref, cos_ref, sin_ref, qn_ref, kvn_ref, wq_ref, wqp_ref, wk_ref, wv_ref,
                     q_o, k_o, v_o):
    cos_t = cos_ref[...]
    sin_t = sin_ref[...]
    cqn = _rms(cq_ref[...], qn_ref[...]).astype(BF16)
    ckvn = _rms(ckv_ref[...], kvn_ref[...]).astype(BF16)
    q = _dot(cqn, wq_ref[...])
    qp = _dot(cqn, wqp_ref[...])
    kn = _dot(ckvn, wk_ref[...])
    v_o[...] = _dot(ckvn, wv_ref[...]).astype(BF16)
    kr = kr_ref[...]
    krot = kr[:, :HEAD_SLOT] * cos_t + kr[:, HEAD_SLOT:] * sin_t
    cos_s = cos_t * ATTN_SCALE
    sin_s = sin_t * ATTN_SCALE
    for h in range(C_HEADS):
        hs = slice(h * HEAD_SLOT, (h + 1) * HEAD_SLOT)
        q_o[0, h] = (q[:, hs] * cos_s + qp[:, hs] * sin_s).astype(BF16)
        k_o[0, h] = (kn[:, hs] + krot).astype(BF16)


def _mla_prep(cq, ckv, kr, cos_t, sin_t, q_norm, kv_norm, wq, wqp, wk, wv, batch, tm=256):
    t = cq.shape[0]
    s = t // batch
    nt = s // tm
    rows = lambda w: pl.BlockSpec((tm, w), lambda b, i: (b * nt + i, 0))
    tab = pl.BlockSpec((tm, HEAD_SLOT), lambda b, i: (i, 0))
    qk_spec = pl.BlockSpec((1, C_HEADS, tm, HEAD_SLOT), lambda b, i: (b, 0, i, 0))
    qk_shape = jax.ShapeDtypeStruct((batch, C_HEADS, s, HEAD_SLOT), BF16)
    return pl.pallas_call(
        _mla_prep_kernel,
        grid=(batch, nt),
        in_specs=[rows(Q_LORA), rows(KV_LORA), rows(2 * HEAD_SLOT), tab, tab, _full((1, Q_LORA)), _full((1, KV_LORA)),
                  _full(wq.shape), _full(wqp.shape), _full(wk.shape), _full(wv.shape)],
        out_specs=[qk_spec, qk_spec, rows(C_HEADS * V_HEAD)],
        out_shape=[qk_shape, qk_shape, jax.ShapeDtypeStruct((t, C_HEADS * V_HEAD), BF16)],
        compiler_params=_params("parallel", "parallel"),
        name="mla_prep",
    )(cq, ckv, kr, cos_t, sin_t, q_norm.reshape(1, -1), kv_norm.reshape(1, -1), wq, wqp, wk, wv)


ATTN_TQ = 512


def _attn_kernel(q_ref, k_ref, v_ref, o_ref):
    tq = q_ref.shape[2]
    qi = pl.program_id(2)
    lane = lax.broadcasted_iota(jnp.int32, (tq, 2 * V_HEAD), 1)
    row_c = lax.broadcasted_iota(jnp.int32, (tq, tq), 0) // CHUNK
    col_c = lax.broadcasted_iota(jnp.int32, (tq, tq), 1) // CHUNK
    diag_keep = col_c <= row_c
    pair = range(2)
    q = [q_ref[0, hh] for hh in pair]

    def block(j, carry, masked):
        m, l, acc = carry
        start = pl.multiple_of(j * tq, tq)
        vb = v_ref[pl.ds(start, tq), :]
        sc = [_dot_nt(q[hh], k_ref[0, hh, pl.ds(start, tq), :]) for hh in pair]
        if masked:
            sc = [jnp.where(diag_keep, s, -jnp.inf) for s in sc]
        m_new = [jnp.maximum(m[hh], jnp.max(sc[hh], axis=-1, keepdims=True)) for hh in pair]
        alpha = [jnp.exp(m[hh] - m_new[hh]) for hh in pair]
        p = [jnp.exp(sc[hh] - m_new[hh]) for hh in pair]
        l = [alpha[hh] * l[hh] + jnp.sum(p[hh], axis=-1, keepdims=True) for hh in pair]
        pv = [_dot(p[hh].astype(BF16), vb) for hh in pair]
        acc = [alpha[hh] * acc[hh] + pv[hh] for hh in pair]
        return m_new, l, acc

    init = ([jnp.full((tq, 1), -jnp.inf, F32)] * 2, [jnp.zeros((tq, 1), F32)] * 2,
            [jnp.zeros((tq, 2 * V_HEAD), F32)] * 2)
    carry = lax.fori_loop(0, qi, functools.partial(block, masked=False), init)
    m, l, acc = block(qi, carry, True)
    o_ref[...] = jnp.where(lane < V_HEAD, acc[0] / l[0], acc[1] / l[1])


def _attention(q, k, v, batch):
    _, heads, s, slot = q.shape
    nq = s // ATTN_TQ
    t = batch * s
    return pl.pallas_call(
        _attn_kernel,
        grid=(batch, heads // 2, nq),
        in_specs=[pl.BlockSpec((1, 2, ATTN_TQ, slot), lambda b, p, i: (b, p, i, 0)),
                  pl.BlockSpec((1, 2, s, slot), lambda b, p, i: (b, p, 0, 0)),
                  pl.BlockSpec((s, 2 * V_HEAD), lambda b, p, i: (b, p))],
        out_specs=pl.BlockSpec((ATTN_TQ, 2 * V_HEAD), lambda b, p, i: (b * nq + i, p)),
        out_shape=jax.ShapeDtypeStruct((t, heads * V_HEAD), F32),
        compiler_params=_params("parallel", "parallel", "arbitrary"),
        name="mla_attention",
    )(q, k, v)


CONV_HALO = 32


def _conv_kernel(zd_ref, w_ref, b_ref, lng_ref, lnb_ref, o_ref, buf_ref, shift_ref):
    tt, d_d = o_ref.shape

    @pl.when(pl.program_id(1) == 0)
    def _():
        buf_ref[0:CONV_HALO, :] = jnp.zeros((CONV_HALO, d_d), F32)

    zd = zd_ref[...]
    buf_ref[CONV_HALO:CONV_HALO + tt, :] = zd[:, :d_d] * _sigmoid(zd[:, d_d:])
    acc = jnp.zeros((tt, d_d), F32) + b_ref[...]
    first = CONV_HALO - (CONV_K - 1)
    for rho in range(SUBLANES):
        taps = [j for j in range(CONV_K) if (first + j) % SUBLANES == rho]
        span = first + taps[-1] - rho + tt
        src = buf_ref
        if rho:
            shift_ref[0:span, :] = buf_ref[rho:rho + span, :]
            src = shift_ref
        for j in taps:
            off = first + j - rho
            acc = acc + w_ref[j:j + 1, :] * src[off:off + tt, :]
    buf_ref[0:CONV_HALO, :] = buf_ref[tt:tt + CONV_HALO, :]
    y = _layernorm(acc, lng_ref[...], lnb_ref[...], LN_EPS)
    o_ref[...] = y * _sigmoid(y)


def _conv(zd, conv_w, conv_b, ln_g, ln_b, batch, tt=256):
    t, two_d = zd.shape
    d_d = two_d // 2
    nt = (t // batch) // tt
    return pl.pallas_call(
        _conv_kernel,
        grid=(batch, nt),
        in_specs=[pl.BlockSpec((tt, two_d), lambda b, i: (b * nt + i, 0)), _full(conv_w.shape),
                  _full((1, d_d)), _full((1, d_d)), _full((1, d_d))],
        out_specs=pl.BlockSpec((tt, d_d), lambda b, i: (b * nt + i, 0)),
        out_shape=jax.ShapeDtypeStruct((t, d_d), F32),
        scratch_shapes=[pltpu.VMEM((tt + CONV_HALO, d_d), F32), pltpu.VMEM((tt + CONV_HALO, d_d), F32)],
        compiler_params=_params("parallel", "arbitrary"),
        name="conv_module",
    )(zd, conv_w, conv_b.reshape(1, d_d), ln_g.reshape(1, d_d), ln_b.reshape(1, d_d))


def _slot_columns(w, per_head, take_lo, take_n):
    rows = w.shape[0]
    wh = w.reshape(rows, C_HEADS, per_head)[:, :, take_lo:take_lo + take_n]
    wh = jnp.pad(wh, ((0, 0), (0, 0), (0, HEAD_SLOT - take_n)))
    return wh.reshape(rows, C_HEADS * HEAD_SLOT)


def _slot_partner(w):
    rows = w.shape[0]
    wh = w.reshape(rows, -1, HEAD_SLOT)
    x1 = wh[:, :, ROPE_LO:ROPE_LO + ROPE_HALF]
    x2 = wh[:, :, ROPE_LO + ROPE_HALF:ROPE_LO + QK_ROPE]
    out = jnp.concatenate([jnp.zeros_like(wh[:, :, :ROPE_LO]), x2, x1,
                           jnp.zeros_like(wh[:, :, ROPE_LO + QK_ROPE:])], axis=2)
    return out.reshape(w.shape)


def _odd_in_weight(w):
    d = w.shape[0]
    lo = Q_LORA + KV_LORA
    kr = jnp.concatenate([jnp.zeros((d, ROPE_LO), w.dtype), w[:, lo:lo + QK_ROPE],
                          jnp.zeros((d, HEAD_SLOT - ROPE_LO - QK_ROPE), w.dtype)], axis=1)
    return jnp.concatenate([w[:, :lo], kr, _slot_partner(kr), w[:, lo + QK_ROPE:]], axis=1)


def kernel(x, norm_ffn1, ffn1_in, ffn1_out, norm_mix, norm_ffn2, ffn2_in, ffn2_out, even_w_in, even_w_out, gsu_ws, gsu_bs, gsu_ln_g, gsu_ln_b, shift_mu, decay_w0, decay_up, iclr_a0, iclr_up, gate_up, k_k, k_a, r_k, lnx_g, lnx_b, odd_w_in, odd_w_out, q_norm, wq_up, kv_norm, wkv_up, conv_w, conv_b, conv_ln_g, conv_ln_b, final_norm):
    batch, seq, d_model = x.shape
    depth = norm_ffn1.shape[0]
    d_a = gsu_ln_g.shape[1]
    d_b = decay_w0.shape[1]
    p_b = shift_mu.shape[1]
    d_d = conv_b.shape[1]
    h = x.reshape(batch * seq, d_model)
    cos_t, sin_t = _rope_tables(seq)
    mixed = None
    for layer in range(depth):
        w1_in, w1_out = ffn1_in[layer].astype(BF16), ffn1_out[layer].astype(BF16)
        w2_in, w2_out = ffn2_in[layer].astype(BF16), ffn2_out[layer].astype(BF16)
        if layer % 2 == 0:
            e = layer // 2
            h, za, zb = _ffn(h, norm_ffn1[layer], w1_in, w1_out,
                             post=(norm_mix[layer], even_w_in[e].astype(BF16), (2 * d_a, p_b)), name="ffn1_even")
            ya = _gsu(za, gsu_ws[e], gsu_bs[e], gsu_ln_g[e], gsu_ln_b[e])
            yb = _rwkv(zb.reshape(batch, seq, p_b), shift_mu[e], decay_w0[e], decay_up[e], iclr_a0[e], iclr_up[e],
                       gate_up[e], k_k[e], k_a[e], r_k[e], lnx_g[e], lnx_b[e]).reshape(batch * seq, d_b)
            mixed = (ya, yb, even_w_out[e].astype(BF16))
        else:
            o = layer // 2
            h, cq, ckv, kr, zd = _ffn(h, norm_ffn1[layer], w1_in, w1_out,
                                      post=(norm_mix[layer], _odd_in_weight(odd_w_in[o]).astype(BF16),
                                            (Q_LORA, KV_LORA, 2 * HEAD_SLOT, 2 * d_d)), name="ffn1_odd")
            wq = _slot_columns(wq_up[o], QK_NOPE + QK_ROPE, 0, QK_NOPE + QK_ROPE)
            wk = _slot_columns(wkv_up[o], QK_NOPE + V_HEAD, 0, QK_NOPE).astype(BF16)
            wv = wkv_up[o].reshape(KV_LORA, C_HEADS, QK_NOPE + V_HEAD)[:, :, QK_NOPE:].reshape(
                KV_LORA, C_HEADS * V_HEAD).astype(BF16)
            q, k, v = _mla_prep(cq, ckv, kr, cos_t, sin_t, q_norm[o], kv_norm[o],
                                wq.astype(BF16), _slot_partner(wq).astype(BF16), wk, wv, batch)
            yc = _attention(q, k, v, batch)
            yd = _conv(zd, conv_w[o], conv_b[o], conv_ln_g[o], conv_ln_b[o], batch)
            mixed = (yc, yd, odd_w_out[o].astype(BF16))
        last = layer == depth - 1
        h = _ffn(h, norm_ffn2[layer], w2_in, w2_out, pre=mixed, fin_g=final_norm if last else None,
                 name="ffn2_final" if last else "ffn2")[0]
    return h.reshape(batch, seq, d_model)
```

```python
import functools
import math

import jax
import jax.numpy as jnp
from jax import lax
from jax.experimental import pallas as pl
from jax.experimental.pallas import tpu as pltpu

F32 = jnp.float32
BF16 = jnp.bfloat16

LANES = 128
SUBLANES = 8
VMEM_LIMIT_BYTES = 56 * 1024 * 1024

CHUNK = 64
RMS_EPS = 1e-6
LN_EPS = 1e-5
LNX_EPS = 64e-5
GSU_BLOCK = 128
A_GROUPS = 4
B_HEAD = 64
LORA_W = 64
LORA_A = 64
LORA_G = 128
C_HEADS = 8
Q_LORA = 256
KV_LORA = 128
QK_NOPE = 64
QK_ROPE = 32
V_HEAD = 64
ROPE_THETA = 10000.0
ATTN_SCALE = (QK_NOPE + QK_ROPE) ** -0.5
CONV_K = 31
HEAD_SLOT = 128
ROPE_LO = QK_NOPE
ROPE_HALF = QK_ROPE // 2


def _params(*sem):
    return pltpu.CompilerParams(dimension_semantics=sem, vmem_limit_bytes=VMEM_LIMIT_BYTES)


def _dot(a, b):
    return jnp.dot(a, b, preferred_element_type=F32)


def _dot_nt(a, b):
    return lax.dot_general(a, b, (((1,), (1,)), ((), ())), preferred_element_type=F32)


def _dot_tn(a, b):
    return lax.dot_general(a, b, (((0,), (0,)), ((), ())), preferred_element_type=F32)


def _split_hi_lo(x):
    hi = x.astype(BF16)
    lo = (x - hi.astype(F32)).astype(BF16)
    return hi, lo


def _dot_exact_rhs(x, m):
    hi, lo = _split_hi_lo(x)
    return _dot(hi, m) + _dot(lo, m)


def _dot_exact_lhs(m, x):
    hi, lo = _split_hi_lo(x)
    return _dot(m, hi) + _dot(m, lo)


def _rms(x, g):
    ms = jnp.mean(x * x, axis=-1, keepdims=True)
    return x * lax.rsqrt(ms + RMS_EPS) * g


def _layernorm(x, g, b, eps):
    mu = jnp.mean(x, axis=-1, keepdims=True)
    xc = x - mu
    var = jnp.mean(xc * xc, axis=-1, keepdims=True)
    return xc * lax.rsqrt(var + eps) * g + b


def _sigmoid(x):
    return 1.0 / (1.0 + jnp.exp(-x))


def _full(shape):
    n = len(shape)
    return pl.BlockSpec(shape, lambda *_: (0,) * n)


FFN_CHUNK = 256


def _ffn_kernel(*refs, d_ff, pre, splits, final):
    refs = list(refs)
    x_ref = refs.pop(0)
    if pre:
        ya_ref, yb_ref, wmo_ref = refs.pop(0), refs.pop(0), refs.pop(0)
    g_ref, win_ref, wout_ref = refs.pop(0), refs.pop(0), refs.pop(0)
    if splits:
        gm_ref, wmi_ref = refs.pop(0), refs.pop(0)
    if final:
        fin_ref = refs.pop(0)
    o_ref = refs.pop(0)
    z_refs = [refs.pop(0) for _ in splits]
    (acc_ref,) = refs

    x = x_ref[...]
    if pre:
        da = ya_ref.shape[1]
        x = x + _dot(ya_ref[...].astype(BF16), wmo_ref[:da, :]) + _dot(yb_ref[...].astype(BF16), wmo_ref[da:, :])
    h = _rms(x, g_ref[...]).astype(BF16)
    for j in range(d_ff // FFN_CHUNK):
        lo = j * FFN_CHUNK
        gate = _dot(h, win_ref[:, lo:lo + FFN_CHUNK])
        up = _dot(h, win_ref[:, d_ff + lo:d_ff + lo + FFN_CHUNK])
        act = (gate * _sigmoid(gate) * up).astype(BF16)
        part = _dot(act, wout_ref[lo:lo + FFN_CHUNK, :])
        if j == 0:
            acc_ref[...] = part
        else:
            acc_ref[...] += part
    y = x + 0.5 * acc_ref[...]
    if final:
        y = _rms(y, fin_ref[...])
    o_ref[...] = y
    if splits:
        h2 = _rms(y, gm_ref[...]).astype(BF16)
        lo = 0
        for z_ref, width in zip(z_refs, splits):
            z_ref[...] = _dot(h2, wmi_ref[:, lo:lo + width]).astype(z_ref.dtype)
            lo += width


def _weight(w):
    if isinstance(w, tuple):
        stack, layer = w
        return pl.BlockSpec((None,) + stack.shape[1:], lambda *_: (layer, 0, 0)), stack
    return _full(w.shape), w


def _ffn(x, g, w_in, w_out, pre=None, post=None, fin_g=None, tm=512, name="ffn"):
    t, d = x.shape
    rows = lambda w: pl.BlockSpec((tm, w), lambda i: (i, 0))
    in_specs, args = [rows(d)], [x]

    def add_weight(w):
        spec, arr = _weight(w)
        in_specs.append(spec)
        args.append(arr)
        return arr.shape[-2:]

    if pre is not None:
        ya, yb, wmo = pre
        in_specs += [rows(ya.shape[1]), rows(yb.shape[1])]
        args += [ya, yb]
        add_weight(wmo)
    in_specs.append(_full((1, d)))
    args.append(g.reshape(1, d))
    add_weight(w_in)
    d_ff = add_weight(w_out)[0]
    splits = ()
    if post is not None:
        gm, wmi, splits = post
        in_specs.append(_full((1, d)))
        args.append(gm.reshape(1, d))
        add_weight(wmi)
    if fin_g is not None:
        in_specs.append(_full((1, d)))
        args.append(fin_g.reshape(1, d))
    out_specs = [rows(d)] + [rows(s) for s in splits]
    out_shape = [jax.ShapeDtypeStruct((t, d), F32)] + [jax.ShapeDtypeStruct((t, s), F32) for s in splits]
    return pl.pallas_call(
        functools.partial(_ffn_kernel, d_ff=d_ff, pre=pre is not None, splits=tuple(splits),
                          final=fin_g is not None),
        grid=(t // tm,),
        in_specs=in_specs,
        out_specs=out_specs,
        out_shape=out_shape,
        scratch_shapes=[pltpu.VMEM((tm, d), F32)],
        compiler_params=_params("parallel"),
        name=name,
    )(*args)


def _gelu_tanh(x):
    c = math.sqrt(2.0 / math.pi)
    return 0.5 * x * (1.0 + jnp.tanh(c * (x + 0.044715 * (x * x * x))))


def _gsu_kernel(za_ref, ws_ref, bias_ref, lng_ref, lnb_ref, o_ref):
    d_a = o_ref.shape[1]
    gd = d_a // A_GROUPS
    g = _gelu_tanh(za_ref[...])
    u = g[:, :d_a]
    v = _layernorm(g[:, d_a:], lng_ref[...], lnb_ref[...], LN_EPS).astype(BF16)
    row = lax.broadcasted_iota(jnp.int32, (GSU_BLOCK, GSU_BLOCK), 0) // CHUNK
    col = lax.broadcasted_iota(jnp.int32, (GSU_BLOCK, GSU_BLOCK), 1) // CHUNK
    keep = row >= col
    for gi in range(A_GROUPS):
        w = jnp.where(keep, ws_ref[gi], 0.0).astype(BF16)
        for blk in range(za_ref.shape[0] // GSU_BLOCK):
            rs = slice(blk * GSU_BLOCK, (blk + 1) * GSU_BLOCK)
            cs = slice(gi * gd, (gi + 1) * gd)
            mixed = _dot(w, v[rs, cs]) + bias_ref[gi]
            o_ref[rs, cs] = u[rs, cs] * mixed


def _gsu(za, ws, bs, ln_g, ln_b, tb=512):
    t, two_da = za.shape
    d_a = two_da // 2
    bias = jnp.broadcast_to(bs[:, :, None], (A_GROUPS, GSU_BLOCK, d_a // A_GROUPS))
    return pl.pallas_call(
        _gsu_kernel,
        grid=(t // tb,),
        in_specs=[pl.BlockSpec((tb, two_da), lambda i: (i, 0)), _full(ws.shape), _full(bias.shape),
                  _full((1, d_a)), _full((1, d_a))],
        out_specs=pl.BlockSpec((tb, d_a), lambda i: (i, 0)),
        out_shape=jax.ShapeDtypeStruct((t, d_a), F32),
        compiler_params=_params("parallel"),
        name="gsu",
    )(za, ws, bias, ln_g.reshape(1, d_a), ln_b.reshape(1, d_a))


RWKV_NB = 4


def _rwkv_kernel(zb_ref, mu_ref, w0_ref, a0_ref, kk_ref, ka_ref, rk_ref, wup_ref, aup_ref, gup_ref,
                 lng_ref, lnb_ref, ones_ref, o_ref, h_ref, buf_ref):
    nb, c, p_b = zb_ref.shape
    d_b = o_ref.shape[2]
    heads = d_b // B_HEAD

    @pl.when(pl.program_id(1) == 0)
    def _():
        h_ref[...] = jnp.zeros(h_ref.shape, F32)
        buf_ref[:, 0:SUBLANES, :] = jnp.zeros((nb, SUBLANES, p_b), F32)

    zbs, prevs = [], []
    for n in range(nb):
        zn = zb_ref[n]
        buf_ref[n, SUBLANES:SUBLANES + c, :] = zn
        prevs.append(buf_ref[n, SUBLANES - 1:SUBLANES - 1 + c, :])
        buf_ref[n, 0:SUBLANES, :] = zn[c - SUBLANES:, :]
        zbs.append(zn)
    zb = jnp.concatenate(zbs, axis=0)
    z = zb + mu_ref[...] * (jnp.concatenate(prevs, axis=0) - zb)

    r = z[:, 0:d_b]
    k = z[:, d_b:2 * d_b]
    v = z[:, 2 * d_b:3 * d_b]
    xwa = z[:, 3 * d_b:3 * d_b + LORA_W + LORA_A]
    xg = z[:, 3 * d_b + LORA_W + LORA_A:]
    lane = lax.broadcasted_iota(jnp.int32, xwa.shape, 1)
    xwa = jnp.where(lane < LORA_W, jnp.tanh(xwa), xwa).astype(BF16)
    ew = _sigmoid(w0_ref[...] + _dot(xwa, wup_ref[...])) * math.exp(-0.5)
    a_lr = _sigmoid(a0_ref[...] + _dot(xwa, aup_ref[...]))
    gate = _dot(_sigmoid(xg).astype(BF16), gup_ref[...])
    ones = ones_ref[...]
    kk = k * kk_ref[...]
    kk = kk / jnp.maximum(jnp.sqrt(_dot_exact_rhs(kk * kk, ones)), 1e-12)
    k = k * (1.0 + (a_lr - 1.0) * ka_ref[...])
    bonus = _dot((r * k * rk_ref[...]).astype(BF16), ones) * v
    a_vec = -kk
    b_vec = kk * a_lr

    ti = lax.broadcasted_iota(jnp.int32, (c, c), 0)
    si = lax.broadcasted_iota(jnp.int32, (c, c), 1)
    tril = (si <= ti).astype(BF16)
    strict = si < ti
    incl = si <= ti
    eye = (ti == si).astype(F32)

    cum = jnp.concatenate([_dot_exact_lhs(tril, ew[n * c:(n + 1) * c]) for n in range(nb)], axis=0)
    total = jnp.concatenate(
        [jnp.broadcast_to(cum[(n + 1) * c - 1:(n + 1) * c], (c, d_b)) for n in range(nb)], axis=0)
    grow = jnp.exp(cum)
    tail = jnp.exp(cum - total)
    a_t = a_vec * jnp.exp(ew - cum)
    r_t = r * jnp.exp(-cum)
    b_t = b_vec * grow
    k_t = k * grow
    b_h = b_vec * tail
    k_h = k * tail
    g_end = [jnp.exp(-cum[(n + 1) * c - 1:(n + 1) * c]) for n in range(nb)]

    bf = lambda t: t.astype(BF16)
    chains = [(n, h) for n in range(nb) for h in range(heads)]
    cr = range(len(chains))
    sel = [(slice(n * c, (n + 1) * c), slice(h * B_HEAD, (h + 1) * B_HEAD)) for n, h in chains]
    cut = lambda t, i: t[sel[i][0], sel[i][1]]
    vh = [bf(cut(v, i)) for i in cr]
    p = [_dot_nt(bf(jnp.concatenate([cut(a_t, i), cut(r_t, i)], axis=0)),
                 bf(jnp.concatenate([cut(b_t, i), cut(k_t, i)], axis=0))) for i in cr]
    nk = [jnp.where(strict, p[i][:c, :c], 0.0) for i in cr]
    a_ak = [bf(jnp.where(strict, p[i][:c, c:], 0.0)) for i in cr]
    m_rb = [bf(jnp.where(incl, p[i][c:, :c], 0.0)) for i in cr]
    m_rk = [bf(jnp.where(incl, p[i][c:, c:], 0.0)) for i in cr]
    x = [jnp.concatenate([cut(a_t, i), _dot(a_ak[i], vh[i])], axis=1) for i in cr]
    y_loc = [_dot(m_rk[i], vh[i]) for i in cr]
    g_loc = [_dot_tn(bf(cut(k_h, i)), vh[i]) for i in cr]
    steps = int(math.log2(c))
    for it in range(steps):
        nkb = [bf(nk[i]) for i in cr]
        x = [x[i] + _dot(nkb[i], bf(x[i])) for i in cr]
        if it + 1 < steps:
            nk = [_dot(nkb[i], nkb[i]) for i in cr]
    xb = [bf(x[i]) for i in cr]
    ry = [_dot(m_rb[i], xb[i]) + jnp.concatenate([cut(r_t, i), y_loc[i]], axis=1) for i in cr]
    pg = [_dot_tn(bf(cut(b_h, i)), xb[i])
          + jnp.concatenate([eye * g_end[chains[i][0]][:, sel[i][1]], g_loc[i]], axis=1) for i in cr]
    h0b = [bf(h_ref[i]) for i in cr]
    ys = [_dot(bf(ry[i][:, :B_HEAD]), h0b[i]) + ry[i][:, B_HEAD:] for i in cr]
    h_new = [_dot(bf(pg[i][:, :B_HEAD]), h0b[i]) + pg[i][:, B_HEAD:] for i in cr]
    for i in cr:
        h_ref[i] = h_new[i]

    y = jnp.concatenate(
        [jnp.concatenate(ys[n * heads:(n + 1) * heads], axis=1) for n in range(nb)], axis=0)
    inv = 1.0 / B_HEAD
    mu = _dot(bf(y), ones) * inv
    yc = y - mu
    var = _dot(bf(yc * yc), ones) * inv
    yn = yc * lax.rsqrt(var + LNX_EPS) * lng_ref[...] + lnb_ref[...]
    out = (yn + bonus) * gate
    for n in range(nb):
        o_ref[n] = out[n * c:(n + 1) * c]


def _head_ones(d, head):
    idx = jnp.arange(d) // head
    return (idx[:, None] == idx[None, :]).astype(BF16)


def _rwkv(zb, shift_mu, decay_w0, decay_up, iclr_a0, iclr_up, gate_up, k_k, k_a, r_k, lnx_g, lnx_b):
    batch, s, p_b = zb.shape
    d_b = decay_w0.shape[0]
    nb = RWKV_NB
    zeros = jnp.zeros((LORA_W, d_b), F32)
    wup = jnp.concatenate([decay_up, zeros], axis=0).astype(BF16)
    aup = jnp.concatenate([zeros, iclr_up], axis=0).astype(BF16)
    row = lambda a: a.reshape(1, -1)
    vec = _full((1, d_b))
    return pl.pallas_call(
        _rwkv_kernel,
        grid=(batch // nb, s // CHUNK),
        in_specs=[pl.BlockSpec((nb, CHUNK, p_b), lambda b, i: (b, i, 0)), _full((1, p_b)),
                  vec, vec, vec, vec, vec, _full(wup.shape), _full(aup.shape), _full((LORA_G, d_b)),
                  vec, vec, _full((d_b, d_b))],
        out_specs=pl.BlockSpec((nb, CHUNK, d_b), lambda b, i: (b, i, 0)),
        out_shape=jax.ShapeDtypeStruct((batch, s, d_b), F32),
        scratch_shapes=[pltpu.VMEM((nb * (d_b // B_HEAD), B_HEAD, B_HEAD), F32),
                        pltpu.VMEM((nb, CHUNK + SUBLANES, p_b), F32)],
        compiler_params=_params("parallel", "arbitrary"),
        name="rwkv",
    )(zb, row(shift_mu), row(decay_w0), row(iclr_a0), row(k_k), row(k_a), row(r_k),
      wup, aup, gate_up.astype(BF16), row(lnx_g), row(lnx_b), _head_ones(d_b, B_HEAD))


def _rope_table_kernel(cos_ref, sin_ref):
    rows = cos_ref.shape[0]
    pos = (pl.program_id(0) * rows + lax.broadcasted_iota(jnp.int32, cos_ref.shape, 0)).astype(F32)
    lane = lax.broadcasted_iota(jnp.int32, cos_ref.shape, 1)
    in_rope = (lane >= ROPE_LO) & (lane < ROPE_LO + QK_ROPE)
    second = lane >= ROPE_LO + ROPE_HALF
    idx = jnp.where(second, lane - ROPE_LO - ROPE_HALF, lane - ROPE_LO).astype(F32)
    inv_freq = jnp.exp(idx * (-2.0 * math.log(ROPE_THETA) / QK_ROPE))
    ang = pos * inv_freq
    cos_ref[...] = jnp.where(in_rope, jnp.cos(ang), 1.0)
    sin_ref[...] = jnp.where(in_rope, jnp.where(second, jnp.sin(ang), -jnp.sin(ang)), 0.0)


def _rope_tables(s, rows=256):
    spec = pl.BlockSpec((rows, HEAD_SLOT), lambda i: (i, 0))
    return pl.pallas_call(
        _rope_table_kernel,
        grid=(s // rows,),
        out_specs=[spec, spec],
        out_shape=[jax.ShapeDtypeStruct((s, HEAD_SLOT), F32)] * 2,
        compiler_params=_params("parallel"),
        name="rope_tables",
    )()


def _mla_prep_kernel(cq_ref, ckv_ref, kr_ref, cos_ref, sin_ref, qn_ref, kvn_ref, wq_ref, wqp_ref, wk_ref, wv_ref,
                     q_o, k_o, v_o):
    cos_t = cos_ref[...]
    sin_t = sin_ref[...]
    cqn = _rms(cq_ref[...], qn_ref[...]).astype(BF16)
    ckvn = _rms(ckv_ref[...], kvn_ref[...]).astype(BF16)
    q = _dot(cqn, wq_ref[...])
    qp = _dot(cqn, wqp_ref[...])
    kn = _dot(ckvn, wk_ref[...])
    v_o[...] = _dot(ckvn, wv_ref[...]).astype(BF16)
    kr = kr_ref[...]
    krot = kr[:, :HEAD_SLOT] * cos_t + kr[:, HEAD_SLOT:] * sin_t
    cos_s = cos_t * ATTN_SCALE
    sin_s = sin_t * ATTN_SCALE
    for h in range(C_HEADS):
        hs = slice(h * HEAD_SLOT, (h + 1) * HEAD_SLOT)
        q_o[0, h] = (q[:, hs] * cos_s + qp[:, hs] * sin_s).astype(BF16)
        k_o[0, h] = (kn[:, hs] + krot).astype(BF16)


def _mla_prep(cq, ckv, kr, cos_t, sin_t, q_norm, kv_norm, wq, wqp, wk, wv, batch, tm=512):
    t = cq.shape[0]
    s = t // batch
    nt = s // tm
    rows = lambda w: pl.BlockSpec((tm, w), lambda b, i: (b * nt + i, 0))
    tab = pl.BlockSpec((tm, HEAD_SLOT), lambda b, i: (i, 0))
    qk_spec = pl.BlockSpec((1, C_HEADS, tm, HEAD_SLOT), lambda b, i: (b, 0, i, 0))
    qk_shape = jax.ShapeDtypeStruct((batch, C_HEADS, s, HEAD_SLOT), BF16)
    return pl.pallas_call(
        _mla_prep_kernel,
        grid=(batch, nt),
        in_specs=[rows(Q_LORA), rows(KV_LORA), rows(2 * HEAD_SLOT), tab, tab, _full((1, Q_LORA)), _full((1, KV_LORA)),
                  _full(wq.shape), _full(wqp.shape), _full(wk.shape), _full(wv.shape)],
        out_specs=[qk_spec, qk_spec, rows(C_HEADS * V_HEAD)],
        out_shape=[qk_shape, qk_shape, jax.ShapeDtypeStruct((t, C_HEADS * V_HEAD), BF16)],
        compiler_params=_params("parallel", "parallel"),
        name="mla_prep",
    )(cq, ckv, kr, cos_t, sin_t, q_norm.reshape(1, -1), kv_norm.reshape(1, -1), wq, wqp, wk, wv)


ATTN_TQ = 512


def _attn_kernel(q_ref, k_ref, v_ref, o_ref):
    tq = q_ref.shape[2]
    qi = pl.program_id(2)
    lane = lax.broadcasted_iota(jnp.int32, (tq, 2 * V_HEAD), 1)
    row_c = lax.broadcasted_iota(jnp.int32, (tq, tq), 0) // CHUNK
    col_c = lax.broadcasted_iota(jnp.int32, (tq, tq), 1) // CHUNK
    diag_keep = col_c <= row_c
    pair = range(2)
    q = [q_ref[0, hh] for hh in pair]

    def block(j, carry, masked):
        m, l, acc = carry
        start = pl.multiple_of(j * tq, tq)
        vb = v_ref[pl.ds(start, tq), :]
        sc = [_dot_nt(q[hh], k_ref[0, hh, pl.ds(start, tq), :]) for hh in pair]
        if masked:
            sc = [jnp.where(diag_keep, s, -jnp.inf) for s in sc]
        m_new = [jnp.maximum(m[hh], jnp.max(sc[hh], axis=-1, keepdims=True)) for hh in pair]
        alpha = [jnp.exp(m[hh] - m_new[hh]) for hh in pair]
        p = [jnp.exp(sc[hh] - m_new[hh]) for hh in pair]
        l = [alpha[hh] * l[hh] + jnp.sum(p[hh], axis=-1, keepdims=True) for hh in pair]
        pv = [_dot(p[hh].astype(BF16), vb) for hh in pair]
        acc = [alpha[hh] * acc[hh] + pv[hh] for hh in pair]
        return m_new, l, acc

    init = ([jnp.full((tq, 1), -jnp.inf, F32)] * 2, [jnp.zeros((tq, 1), F32)] * 2,
            [jnp.zeros((tq, 2 * V_HEAD), F32)] * 2)
    carry = lax.fori_loop(0, qi, functools.partial(block, masked=False), init)
    m, l, acc = block(qi, carry, True)
    o_ref[...] = jnp.where(lane < V_HEAD, acc[0] / l[0], acc[1] / l[1])


def _attention(q, k, v, batch):
    _, heads, s, slot = q.shape
    nq = s // ATTN_TQ
    t = batch * s
    return pl.pallas_call(
        _attn_kernel,
        grid=(batch, heads // 2, nq),
        in_specs=[pl.BlockSpec((1, 2, ATTN_TQ, slot), lambda b, p, i: (b, p, i, 0)),
                  pl.BlockSpec((1, 2, s, slot), lambda b, p, i: (b, p, 0, 0)),
                  pl.BlockSpec((s, 2 * V_HEAD), lambda b, p, i: (b, p))],
        out_specs=pl.BlockSpec((ATTN_TQ, 2 * V_HEAD), lambda b, p, i: (b * nq + i, p)),
        out_shape=jax.ShapeDtypeStruct((t, heads * V_HEAD), F32),
        compiler_params=_params("parallel", "parallel", "arbitrary"),
        name="mla_attention",
    )(q, k, v)


CONV_HALO = 32


def _conv_kernel(zd_ref, w_ref, b_ref, lng_ref, lnb_ref, o_ref, buf_ref, shift_ref):
    tt, d_d = o_ref.shape

    @pl.when(pl.program_id(1) == 0)
    def _():
        buf_ref[0:CONV_HALO, :] = jnp.zeros((CONV_HALO, d_d), F32)

    zd = zd_ref[...]
    buf_ref[CONV_HALO:CONV_HALO + tt, :] = zd[:, :d_d] * _sigmoid(zd[:, d_d:])
    acc = jnp.zeros((tt, d_d), F32) + b_ref[...]
    first = CONV_HALO - (CONV_K - 1)
    for rho in range(SUBLANES):
        taps = [j for j in range(CONV_K) if (first + j) % SUBLANES == rho]
        span = first + taps[-1] - rho + tt
        src = buf_ref
        if rho:
            shift_ref[0:span, :] = buf_ref[rho:rho + span, :]
            src = shift_ref
        for j in taps:
            off = first + j - rho
            acc = acc + w_ref[j:j + 1, :] * src[off:off + tt, :]
    buf_ref[0:CONV_HALO, :] = buf_ref[tt:tt + CONV_HALO, :]
    y = _layernorm(acc, lng_ref[...], lnb_ref[...], LN_EPS)
    o_ref[...] = y * _sigmoid(y)


def _conv(zd, conv_w, conv_b, ln_g, ln_b, batch, tt=512):
    t, two_d = zd.shape
    d_d = two_d // 2
    nt = (t // batch) // tt
    return pl.pallas_call(
        _conv_kernel,
        grid=(batch, nt),
        in_specs=[pl.BlockSpec((tt, two_d), lambda b, i: (b * nt + i, 0)), _full(conv_w.shape),
                  _full((1, d_d)), _full((1, d_d)), _full((1, d_d))],
        out_specs=pl.BlockSpec((tt, d_d), lambda b, i: (b * nt + i, 0)),
        out_shape=jax.ShapeDtypeStruct((t, d_d), F32),
        scratch_shapes=[pltpu.VMEM((tt + CONV_HALO, d_d), F32), pltpu.VMEM((tt + CONV_HALO, d_d), F32)],
        compiler_params=_params("parallel", "arbitrary"),
        name="conv_module",
    )(zd, conv_w, conv_b.reshape(1, d_d), ln_g.reshape(1, d_d), ln_b.reshape(1, d_d))


def _slot_columns(w, per_head, take_lo, take_n):
    rows = w.shape[0]
    wh = w.reshape(rows, C_HEADS, per_head)[:, :, take_lo:take_lo + take_n]
    wh = jnp.pad(wh, ((0, 0), (0, 0), (0, HEAD_SLOT - take_n)))
    return wh.reshape(rows, C_HEADS * HEAD_SLOT)


def _slot_partner(w):
    rows = w.shape[0]
    wh = w.reshape(rows, -1, HEAD_SLOT)
    x1 = wh[:, :, ROPE_LO:ROPE_LO + ROPE_HALF]
    x2 = wh[:, :, ROPE_LO + ROPE_HALF:ROPE_LO + QK_ROPE]
    out = jnp.concatenate([jnp.zeros_like(wh[:, :, :ROPE_LO]), x2, x1,
                           jnp.zeros_like(wh[:, :, ROPE_LO + QK_ROPE:])], axis=2)
    return out.reshape(w.shape)


def _odd_in_weight(w):
    d = w.shape[0]
    lo = Q_LORA + KV_LORA
    kr = jnp.concatenate([jnp.zeros((d, ROPE_LO), w.dtype), w[:, lo:lo + QK_ROPE],
                          jnp.zeros((d, HEAD_SLOT - ROPE_LO - QK_ROPE), w.dtype)], axis=1)
    return jnp.concatenate([w[:, :lo], kr, _slot_partner(kr), w[:, lo + QK_ROPE:]], axis=1)


def kernel(x, norm_ffn1, ffn1_in, ffn1_out, norm_mix, norm_ffn2, ffn2_in, ffn2_out, even_w_in, even_w_out, gsu_ws, gsu_bs, gsu_ln_g, gsu_ln_b, shift_mu, decay_w0, decay_up, iclr_a0, iclr_up, gate_up, k_k, k_a, r_k, lnx_g, lnx_b, odd_w_in, odd_w_out, q_norm, wq_up, kv_norm, wkv_up, conv_w, conv_b, conv_ln_g, conv_ln_b, final_norm):
    batch, seq, d_model = x.shape
    depth = norm_ffn1.shape[0]
    d_a = gsu_ln_g.shape[1]
    d_b = decay_w0.shape[1]
    p_b = shift_mu.shape[1]
    d_d = conv_b.shape[1]
    h = x.reshape(batch * seq, d_model)
    cos_t, sin_t = _rope_tables(seq)
    mixed = None
    ffn1_in, ffn1_out, ffn2_in, ffn2_out, even_w_in, even_w_out, odd_w_out = (
        w.astype(BF16) for w in (ffn1_in, ffn1_out, ffn2_in, ffn2_out, even_w_in, even_w_out, odd_w_out))
    for layer in range(depth):
        w1_in, w1_out = (ffn1_in, layer), (ffn1_out, layer)
        w2_in, w2_out = (ffn2_in, layer), (ffn2_out, layer)
        if layer % 2 == 0:
            e = layer // 2
            h, za, zb = _ffn(h, norm_ffn1[layer], w1_in, w1_out,
                             post=(norm_mix[layer], (even_w_in, e), (2 * d_a, p_b)), name="ffn1_even")
            ya = _gsu(za, gsu_ws[e], gsu_bs[e], gsu_ln_g[e], gsu_ln_b[e])
            yb = _rwkv(zb.reshape(batch, seq, p_b), shift_mu[e], decay_w0[e], decay_up[e], iclr_a0[e], iclr_up[e],
                       gate_up[e], k_k[e], k_a[e], r_k[e], lnx_g[e], lnx_b[e]).reshape(batch * seq, d_b)
            mixed = (ya, yb, (even_w_out, e))
        else:
            o = layer // 2
            h, cq, ckv, kr, zd = _ffn(h, norm_ffn1[layer], w1_in, w1_out,
                                      post=(norm_mix[layer], _odd_in_weight(odd_w_in[o]).astype(BF16),
                                            (Q_LORA, KV_LORA, 2 * HEAD_SLOT, 2 * d_d)), name="ffn1_odd")
            wq = _slot_columns(wq_up[o], QK_NOPE + QK_ROPE, 0, QK_NOPE + QK_ROPE)
            wk = _slot_columns(wkv_up[o], QK_NOPE + V_HEAD, 0, QK_NOPE).astype(BF16)
            wv = wkv_up[o].reshape(KV_LORA, C_HEADS, QK_NOPE + V_HEAD)[:, :, QK_NOPE:].reshape(
                KV_LORA, C_HEADS * V_HEAD).astype(BF16)
            q, k, v = _mla_prep(cq, ckv, kr, cos_t, sin_t, q_norm[o], kv_norm[o],
                                wq.astype(BF16), _slot_partner(wq).astype(BF16), wk, wv, batch)
            yc = _attention(q, k, v, batch)
            yd = _conv(zd, conv_w[o], conv_b[o], conv_ln_g[o], conv_ln_b[o], batch)
            mixed = (yc, yd, (odd_w_out, o))
        last = layer == depth - 1
        h = _ffn(h, norm_ffn2[layer], w2_in, w2_out, pre=mixed, fin_g=final_norm if last else None,
                 name="ffn2_final" if last else "ffn2")[0]
    return h.reshape(batch, seq, d_model)
```

```python
import functools
import math

import jax
import jax.numpy as jnp
from jax import lax
from jax.experimental import pallas as pl
from jax.experimental.pallas import tpu as pltpu

F32 = jnp.float32
BF16 = jnp.bfloat16

LANES = 128
SUBLANES = 8
VMEM_LIMIT_BYTES = 56 * 1024 * 1024

CHUNK = 64
RMS_EPS = 1e-6
LN_EPS = 1e-5
LNX_EPS = 64e-5
GSU_BLOCK = 128
A_GROUPS = 4
B_HEAD = 64
LORA_W = 64
LORA_A = 64
LORA_G = 128
C_HEADS = 8
Q_LORA = 256
KV_LORA = 128
QK_NOPE = 64
QK_ROPE = 32
V_HEAD = 64
ROPE_THETA = 10000.0
ATTN_SCALE = (QK_NOPE + QK_ROPE) ** -0.5
CONV_K = 31
HEAD_SLOT = 128
ROPE_LO = QK_NOPE
ROPE_HALF = QK_ROPE // 2


def _params(*sem):
    return pltpu.CompilerParams(dimension_semantics=sem, vmem_limit_bytes=VMEM_LIMIT_BYTES)


def _dot(a, b):
    return jnp.dot(a, b, preferred_element_type=F32)


def _dot_nt(a, b):
    return lax.dot_general(a, b, (((1,), (1,)), ((), ())), preferred_element_type=F32)


def _dot_tn(a, b):
    return lax.dot_general(a, b, (((0,), (0,)), ((), ())), preferred_element_type=F32)


def _split_hi_lo(x):
    hi = x.astype(BF16)
    lo = (x - hi.astype(F32)).astype(BF16)
    return hi, lo


def _dot_exact_rhs(x, m):
    hi, lo = _split_hi_lo(x)
    return _dot(hi, m) + _dot(lo, m)


def _dot_exact_lhs(m, x):
    hi, lo = _split_hi_lo(x)
    return _dot(m, hi) + _dot(m, lo)


def _rms(x, g):
    ms = jnp.mean(x * x, axis=-1, keepdims=True)
    return x * lax.rsqrt(ms + RMS_EPS) * g


def _layernorm(x, g, b, eps):
    mu = jnp.mean(x, axis=-1, keepdims=True)
    xc = x - mu
    var = jnp.mean(xc * xc, axis=-1, keepdims=True)
    return xc * lax.rsqrt(var + eps) * g + b


def _sigmoid(x):
    return 1.0 / (1.0 + jnp.exp(-x))


def _full(shape):
    n = len(shape)
    return pl.BlockSpec(shape, lambda *_: (0,) * n)


FFN_CHUNK = 256


MLA_SPLITS = 3


def _ffn_kernel(*refs, d_ff, pre, splits, gsu, mla, final):
    refs = list(refs)
    x_ref = refs.pop(0)
    if pre:
        ya_ref, yb_ref, wmo_ref = refs.pop(0), refs.pop(0), refs.pop(0)
    g_ref, win_ref, wout_ref = refs.pop(0), refs.pop(0), refs.pop(0)
    if splits:
        gm_ref, wmi_ref = refs.pop(0), refs.pop(0)
    if gsu:
        gsu_refs = [refs.pop(0) for _ in range(4)]
    if mla:
        mla_refs = [refs.pop(0) for _ in range(8)]
    if final:
        fin_ref = refs.pop(0)
    o_ref = refs.pop(0)
    if mla:
        mla_refs += [refs.pop(0) for _ in range(3)]
    z_refs = [None] * (MLA_SPLITS if mla else 0)
    z_refs += [refs.pop(0) for _ in range(len(splits) - len(z_refs))]
    (acc_ref,) = refs

    x = x_ref[...]
    if pre:
        da = ya_ref.shape[1]
        x = x + _dot(ya_ref[...].astype(BF16), wmo_ref[:da, :]) + _dot(yb_ref[...].astype(BF16), wmo_ref[da:, :])
    h = _rms(x, g_ref[...]).astype(BF16)
    for j in range(d_ff // FFN_CHUNK):
        lo = j * FFN_CHUNK
        gate = _dot(h, win_ref[:, lo:lo + FFN_CHUNK])
        up = _dot(h, win_ref[:, d_ff + lo:d_ff + lo + FFN_CHUNK])
        act = (gate * _sigmoid(gate) * up).astype(BF16)
        part = _dot(act, wout_ref[lo:lo + FFN_CHUNK, :])
        if j == 0:
            acc_ref[...] = part
        else:
            acc_ref[...] += part
    y = x + 0.5 * acc_ref[...]
    if final:
        y = _rms(y, fin_ref[...])
    o_ref[...] = y
    if splits:
        h2 = _rms(y, gm_ref[...]).astype(BF16)
        lo = 0
        latent = []
        for n, (z_ref, width) in enumerate(zip(z_refs, splits)):
            z = _dot(h2, wmi_ref[:, lo:lo + width])
            if gsu and n == 0:
                _gsu_mix(z, *gsu_refs, z_ref)
            elif z_ref is None:
                latent.append(z)
            else:
                z_ref[...] = z
            lo += width
        if mla:
            _mla_prep(*latent, *mla_refs)


def _weight(w):
    if isinstance(w, tuple):
        stack, layer = w
        return pl.BlockSpec((None,) + stack.shape[1:], lambda *_: (layer, 0, 0)), stack
    return _full(w.shape), w


def _ffn(x, g, w_in, w_out, pre=None, post=None, fin_g=None, tm=512, name="ffn"):
    t, d = x.shape
    rows = lambda w: pl.BlockSpec((tm, w), lambda i: (i, 0))
    in_specs, args = [rows(d)], [x]

    def add_weight(w):
        spec, arr = _weight(w)
        in_specs.append(spec)
        args.append(arr)
        return arr.shape[-2:]

    if pre is not None:
        ya, yb, wmo = pre
        in_specs += [rows(ya.shape[1]), rows(yb.shape[1])]
        args += [ya, yb]
        add_weight(wmo)
    in_specs.append(_full((1, d)))
    args.append(g.reshape(1, d))
    add_weight(w_in)
    d_ff = add_weight(w_out)[0]
    splits, gsu, mla = (), None, None
    if post is not None:
        gm, wmi, splits, gsu, mla = post
        in_specs.append(_full((1, d)))
        args.append(gm.reshape(1, d))
        add_weight(wmi)
    out_widths = list(splits)
    if gsu is not None:
        ws, bs, ln_g, ln_b = gsu
        d_a = splits[0] // 2
        bias = jnp.broadcast_to(bs[:, :, None], (A_GROUPS, GSU_BLOCK, d_a // A_GROUPS))
        in_specs += [_full(ws.shape), _full(bias.shape), _full((1, d_a)), _full((1, d_a))]
        args += [ws, bias, ln_g.reshape(1, d_a), ln_b.reshape(1, d_a)]
        out_widths[0] = d_a
    out_specs = [rows(d)]
    out_shape = [jax.ShapeDtypeStruct((t, d), F32)]
    if mla is not None:
        cos_t, sin_t, q_norm, kv_norm, wq, wqp, wk, wv, batch = mla
        s = t // batch
        nt = s // tm
        tab = pl.BlockSpec((tm, HEAD_SLOT), lambda i: (i % nt, 0))
        in_specs += [tab, tab, _full((1, Q_LORA)), _full((1, KV_LORA)),
                     _full(wq.shape), _full(wqp.shape), _full(wk.shape), _full(wv.shape)]
        args += [cos_t, sin_t, q_norm.reshape(1, -1), kv_norm.reshape(1, -1), wq, wqp, wk, wv]
        qk_spec = pl.BlockSpec((1, C_HEADS, tm, HEAD_SLOT), lambda i: (i // nt, 0, i % nt, 0))
        qk_shape = jax.ShapeDtypeStruct((batch, C_HEADS, s, HEAD_SLOT), BF16)
        out_specs += [qk_spec, qk_spec, rows(C_HEADS * V_HEAD)]
        out_shape += [qk_shape, qk_shape, jax.ShapeDtypeStruct((t, C_HEADS * V_HEAD), BF16)]
        out_widths = out_widths[MLA_SPLITS:]
    if fin_g is not None:
        in_specs.append(_full((1, d)))
        args.append(fin_g.reshape(1, d))
    out_specs += [rows(s) for s in out_widths]
    out_shape += [jax.ShapeDtypeStruct((t, s), F32) for s in out_widths]
    return pl.pallas_call(
        functools.partial(_ffn_kernel, d_ff=d_ff, pre=pre is not None, splits=tuple(splits),
                          gsu=gsu is not None, mla=mla is not None, final=fin_g is not None),
        grid=(t // tm,),
        in_specs=in_specs,
        out_specs=out_specs,
        out_shape=out_shape,
        scratch_shapes=[pltpu.VMEM((tm, d), F32)],
        compiler_params=_params("parallel"),
        name=name,
    )(*args)


def _gelu_tanh(x):
    c = math.sqrt(2.0 / math.pi)
    return 0.5 * x * (1.0 + jnp.tanh(c * (x + 0.044715 * (x * x * x))))


def _gsu_mix(za, ws_ref, bias_ref, lng_ref, lnb_ref, o_ref):
    d_a = o_ref.shape[1]
    gd = d_a // A_GROUPS
    g = _gelu_tanh(za)
    u = g[:, :d_a]
    v = _layernorm(g[:, d_a:], lng_ref[...], lnb_ref[...], LN_EPS).astype(BF16)
    row = lax.broadcasted_iota(jnp.int32, (GSU_BLOCK, GSU_BLOCK), 0) // CHUNK
    col = lax.broadcasted_iota(jnp.int32, (GSU_BLOCK, GSU_BLOCK), 1) // CHUNK
    keep = row >= col
    for gi in range(A_GROUPS):
        w = jnp.where(keep, ws_ref[gi], 0.0).astype(BF16)
        for blk in range(za.shape[0] // GSU_BLOCK):
            rs = slice(blk * GSU_BLOCK, (blk + 1) * GSU_BLOCK)
            cs = slice(gi * gd, (gi + 1) * gd)
            mixed = _dot(w, v[rs, cs]) + bias_ref[gi]
            o_ref[rs, cs] = u[rs, cs] * mixed


RWKV_NB = 4


def _rwkv_kernel(zb_ref, mu_ref, w0_ref, a0_ref, kk_ref, ka_ref, rk_ref, wup_ref, aup_ref, gup_ref,
                 lng_ref, lnb_ref, ones_ref, o_ref, h_ref, buf_ref):
    nb, c, p_b = zb_ref.shape
    d_b = o_ref.shape[2]
    heads = d_b // B_HEAD

    @pl.when(pl.program_id(1) == 0)
    def _():
        h_ref[...] = jnp.zeros(h_ref.shape, F32)
        buf_ref[:, 0:SUBLANES, :] = jnp.zeros((nb, SUBLANES, p_b), F32)

    zbs, prevs = [], []
    for n in range(nb):
        zn = zb_ref[n]
        buf_ref[n, SUBLANES:SUBLANES + c, :] = zn
        prevs.append(buf_ref[n, SUBLANES - 1:SUBLANES - 1 + c, :])
        buf_ref[n, 0:SUBLANES, :] = zn[c - SUBLANES:, :]
        zbs.append(zn)
    zb = jnp.concatenate(zbs, axis=0)
    z = zb + mu_ref[...] * (jnp.concatenate(prevs, axis=0) - zb)

    r = z[:, 0:d_b]
    k = z[:, d_b:2 * d_b]
    v = z[:, 2 * d_b:3 * d_b]
    xwa = z[:, 3 * d_b:3 * d_b + LORA_W + LORA_A]
    xg = z[:, 3 * d_b + LORA_W + LORA_A:]
    lane = lax.broadcasted_iota(jnp.int32, xwa.shape, 1)
    xwa = jnp.where(lane < LORA_W, jnp.tanh(xwa), xwa).astype(BF16)
    ew = _sigmoid(w0_ref[...] + _dot(xwa, wup_ref[...])) * math.exp(-0.5)
    a_lr = _sigmoid(a0_ref[...] + _dot(xwa, aup_ref[...]))
    gate = _dot(_sigmoid(xg).astype(BF16), gup_ref[...])
    ones = ones_ref[...]
    kk = k * kk_ref[...]
    kk = kk / jnp.maximum(jnp.sqrt(_dot_exact_rhs(kk * kk, ones)), 1e-12)
    k = k * (1.0 + (a_lr - 1.0) * ka_ref[...])
    bonus = _dot((r * k * rk_ref[...]).astype(BF16), ones) * v
    a_vec = -kk
    b_vec = kk * a_lr

    ti = lax.broadcasted_iota(jnp.int32, (c, c), 0)
    si = lax.broadcasted_iota(jnp.int32, (c, c), 1)
    tril = (si <= ti).astype(BF16)
    strict = si < ti
    incl = si <= ti
    eye = (ti == si).astype(F32)

    cum = jnp.concatenate([_dot_exact_lhs(tril, ew[n * c:(n + 1) * c]) for n in range(nb)], axis=0)
    total = jnp.concatenate(
        [jnp.broadcast_to(cum[(n + 1) * c - 1:(n + 1) * c], (c, d_b)) for n in range(nb)], axis=0)
    grow = jnp.exp(cum)
    tail = jnp.exp(cum - total)
    a_t = a_vec * jnp.exp(ew - cum)
    r_t = r * jnp.exp(-cum)
    b_t = b_vec * grow
    k_t = k * grow
    b_h = b_vec * tail
    k_h = k * tail
    g_end = [jnp.exp(-cum[(n + 1) * c - 1:(n + 1) * c]) for n in range(nb)]

    bf = lambda t: t.astype(BF16)
    chains = [(n, h) for n in range(nb) for h in range(heads)]
    cr = range(len(chains))
    sel = [(slice(n * c, (n + 1) * c), slice(h * B_HEAD, (h + 1) * B_HEAD)) for n, h in chains]
    cut = lambda t, i: t[sel[i][0], sel[i][1]]
    vh = [bf(cut(v, i)) for i in cr]
    p = [_dot_nt(bf(jnp.concatenate([cut(a_t, i), cut(r_t, i)], axis=0)),
                 bf(jnp.concatenate([cut(b_t, i), cut(k_t, i)], axis=0))) for i in cr]
    nk = [jnp.where(strict, p[i][:c, :c], 0.0) for i in cr]
    a_ak = [bf(jnp.where(strict, p[i][:c, c:], 0.0)) for i in cr]
    m_rb = [bf(jnp.where(incl, p[i][c:, :c], 0.0)) for i in cr]
    m_rk = [bf(jnp.where(incl, p[i][c:, c:], 0.0)) for i in cr]
    x = [jnp.concatenate([cut(a_t, i), _dot(a_ak[i], vh[i])], axis=1) for i in cr]
    y_loc = [_dot(m_rk[i], vh[i]) for i in cr]
    g_loc = [_dot_tn(bf(cut(k_h, i)), vh[i]) for i in cr]
    steps = int(math.log2(c))
    for it in range(steps):
        nkb = [bf(nk[i]) for i in cr]
        x = [x[i] + _dot(nkb[i], bf(x[i])) for i in cr]
        if it + 1 < steps:
            nk = [_dot(nkb[i], nkb[i]) for i in cr]
    xb = [bf(x[i]) for i in cr]
    ry = [_dot(m_rb[i], xb[i]) + jnp.concatenate([cut(r_t, i), y_loc[i]], axis=1) for i in cr]
    pg = [_dot_tn(bf(cut(b_h, i)), xb[i])
          + jnp.concatenate([eye * g_end[chains[i][0]][:, sel[i][1]], g_loc[i]], axis=1) for i in cr]
    h0b = [bf(h_ref[i]) for i in cr]
    ys = [_dot(bf(ry[i][:, :B_HEAD]), h0b[i]) + ry[i][:, B_HEAD:] for i in cr]
    h_new = [_dot(bf(pg[i][:, :B_HEAD]), h0b[i]) + pg[i][:, B_HEAD:] for i in cr]
    for i in cr:
        h_ref[i] = h_new[i]

    y = jnp.concatenate(
        [jnp.concatenate(ys[n * heads:(n + 1) * heads], axis=1) for n in range(nb)], axis=0)
    inv = 1.0 / B_HEAD
    mu = _dot(bf(y), ones) * inv
    yc = y - mu
    var = _dot(bf(yc * yc), ones) * inv
    yn = yc * lax.rsqrt(var + LNX_EPS) * lng_ref[...] + lnb_ref[...]
    out = (yn + bonus) * gate
    for n in range(nb):
        o_ref[n] = out[n * c:(n + 1) * c]


def _head_ones(d, head):
    idx = jnp.arange(d) // head
    return (idx[:, None] == idx[None, :]).astype(BF16)


def _rwkv(zb, shift_mu, decay_w0, decay_up, iclr_a0, iclr_up, gate_up, k_k, k_a, r_k, lnx_g, lnx_b):
    batch, s, p_b = zb.shape
    d_b = decay_w0.shape[0]
    nb = RWKV_NB
    zeros = jnp.zeros((LORA_W, d_b), F32)
    wup = jnp.concatenate([decay_up, zeros], axis=0).astype(BF16)
    aup = jnp.concatenate([zeros, iclr_up], axis=0).astype(BF16)
    row = lambda a: a.reshape(1, -1)
    vec = _full((1, d_b))
    return pl.pallas_call(
        _rwkv_kernel,
        grid=(batch // nb, s // CHUNK),
        in_specs=[pl.BlockSpec((nb, CHUNK, p_b), lambda b, i: (b, i, 0)), _full((1, p_b)),
                  vec, vec, vec, vec, vec, _full(wup.shape), _full(aup.shape), _full((LORA_G, d_b)),
                  vec, vec, _full((d_b, d_b))],
        out_specs=pl.BlockSpec((nb, CHUNK, d_b), lambda b, i: (b, i, 0)),
        out_shape=jax.ShapeDtypeStruct((batch, s, d_b), F32),
        scratch_shapes=[pltpu.VMEM((nb * (d_b // B_HEAD), B_HEAD, B_HEAD), F32),
                        pltpu.VMEM((nb, CHUNK + SUBLANES, p_b), F32)],
        compiler_params=_params("parallel", "arbitrary"),
        name="rwkv",
    )(zb, row(shift_mu), row(decay_w0), row(iclr_a0), row(k_k), row(k_a), row(r_k),
      wup, aup, gate_up.astype(BF16), row(lnx_g), row(lnx_b), _head_ones(d_b, B_HEAD))


def _rope_table_kernel(cos_ref, sin_ref):
    rows = cos_ref.shape[0]
    pos = (pl.program_id(0) * rows + lax.broadcasted_iota(jnp.int32, cos_ref.shape, 0)).astype(F32)
    lane = lax.broadcasted_iota(jnp.int32, cos_ref.shape, 1)
    in_rope = (lane >= ROPE_LO) & (lane < ROPE_LO + QK_ROPE)
    second = lane >= ROPE_LO + ROPE_HALF
    idx = jnp.where(second, lane - ROPE_LO - ROPE_HALF, lane - ROPE_LO).astype(F32)
    inv_freq = jnp.exp(idx * (-2.0 * math.log(ROPE_THETA) / QK_ROPE))
    ang = pos * inv_freq
    cos_ref[...] = jnp.where(in_rope, jnp.cos(ang), 1.0)
    sin_ref[...] = jnp.where(in_rope, jnp.where(second, jnp.sin(ang), -jnp.sin(ang)), 0.0)


def _rope_tables(s, rows=256):
    spec = pl.BlockSpec((rows, HEAD_SLOT), lambda i: (i, 0))
    return pl.pallas_call(
        _rope_table_kernel,
        grid=(s // rows,),
        out_specs=[spec, spec],
        out_shape=[jax.ShapeDtypeStruct((s, HEAD_SLOT), F32)] * 2,
        compiler_params=_params("parallel"),
        name="rope_tables",
    )()


def _mla_prep(cq, ckv, kr, cos_ref, sin_ref, qn_ref, kvn_ref, wq_ref, wqp_ref, wk_ref, wv_ref, q_o, k_o, v_o):
    cos_t = cos_ref[...]
    sin_t = sin_ref[...]
    cqn = _rms(cq, qn_ref[...]).astype(BF16)
    ckvn = _rms(ckv, kvn_ref[...]).astype(BF16)
    q = _dot(cqn, wq_ref[...])
    qp = _dot(cqn, wqp_ref[...])
    kn = _dot(ckvn, wk_ref[...])
    v_o[...] = _dot(ckvn, wv_ref[...]).astype(BF16)
    krot = kr[:, :HEAD_SLOT] * cos_t + kr[:, HEAD_SLOT:] * sin_t
    cos_s = cos_t * ATTN_SCALE
    sin_s = sin_t * ATTN_SCALE
    for h in range(C_HEADS):
        hs = slice(h * HEAD_SLOT, (h + 1) * HEAD_SLOT)
        q_o[0, h] = (q[:, hs] * cos_s + qp[:, hs] * sin_s).astype(BF16)
        k_o[0, h] = (kn[:, hs] + krot).astype(BF16)


ATTN_TQ = 512


def _attn_kernel(q_ref, k_ref, v_ref, o_ref):
    tq = q_ref.shape[2]
    qi = pl.program_id(2)
    lane = lax.broadcasted_iota(jnp.int32, (tq, 2 * V_HEAD), 1)
    row_c = lax.broadcasted_iota(jnp.int32, (tq, tq), 0) // CHUNK
    col_c = lax.broadcasted_iota(jnp.int32, (tq, tq), 1) // CHUNK
    diag_keep = col_c <= row_c
    pair = range(2)
    q = [q_ref[0, hh] for hh in pair]

    def block(j, carry, masked):
        m, l, acc = carry
        start = pl.multiple_of(j * tq, tq)
        vb = v_ref[pl.ds(start, tq), :]
        sc = [_dot_nt(q[hh], k_ref[0, hh, pl.ds(start, tq), :]) for hh in pair]
        if masked:
            sc = [jnp.where(diag_keep, s, -jnp.inf) for s in sc]
        m_new = [jnp.maximum(m[hh], jnp.max(sc[hh], axis=-1, keepdims=True)) for hh in pair]
        alpha = [jnp.exp(m[hh] - m_new[hh]) for hh in pair]
        p = [jnp.exp(sc[hh] - m_new[hh]) for hh in pair]
        l = [alpha[hh] * l[hh] + jnp.sum(p[hh], axis=-1, keepdims=True) for hh in pair]
        pv = [_dot(p[hh].astype(BF16), vb) for hh in pair]
        acc = [alpha[hh] * acc[hh] + pv[hh] for hh in pair]
        return m_new, l, acc

    init = ([jnp.full((tq, 1), -jnp.inf, F32)] * 2, [jnp.zeros((tq, 1), F32)] * 2,
            [jnp.zeros((tq, 2 * V_HEAD), F32)] * 2)
    carry = lax.fori_loop(0, qi, functools.partial(block, masked=False), init)
    m, l, acc = block(qi, carry, True)
    o_ref[...] = jnp.where(lane < V_HEAD, acc[0] / l[0], acc[1] / l[1])


def _attention(q, k, v, batch):
    _, heads, s, slot = q.shape
    nq = s // ATTN_TQ
    t = batch * s
    return pl.pallas_call(
        _attn_kernel,
        grid=(batch, heads // 2, nq),
        in_specs=[pl.BlockSpec((1, 2, ATTN_TQ, slot), lambda b, p, i: (b, p, i, 0)),
                  pl.BlockSpec((1, 2, s, slot), lambda b, p, i: (b, p, 0, 0)),
                  pl.BlockSpec((s, 2 * V_HEAD), lambda b, p, i: (b, p))],
        out_specs=pl.BlockSpec((ATTN_TQ, 2 * V_HEAD), lambda b, p, i: (b * nq + i, p)),
        out_shape=jax.ShapeDtypeStruct((t, heads * V_HEAD), F32),
        compiler_params=_params("parallel", "parallel", "arbitrary"),
        name="mla_attention",
    )(q, k, v)


CONV_HALO = 32


def _conv_kernel(zd_ref, w_ref, b_ref, lng_ref, lnb_ref, o_ref, buf_ref, shift_ref):
    tt, d_d = o_ref.shape

    @pl.when(pl.program_id(1) == 0)
    def _():
        buf_ref[0:CONV_HALO, :] = jnp.zeros((CONV_HALO, d_d), F32)

    zd = zd_ref[...]
    buf_ref[CONV_HALO:CONV_HALO + tt, :] = zd[:, :d_d] * _sigmoid(zd[:, d_d:])
    acc = jnp.zeros((tt, d_d), F32) + b_ref[...]
    first = CONV_HALO - (CONV_K - 1)
    for rho in range(SUBLANES):
        taps = [j for j in range(CONV_K) if (first + j) % SUBLANES == rho]
        span = first + taps[-1] - rho + tt
        src = buf_ref
        if rho:
            shift_ref[0:span, :] = buf_ref[rho:rho + span, :]
            src = shift_ref
        for j in taps:
            off = first + j - rho
            acc = acc + w_ref[j:j + 1, :] * src[off:off + tt, :]
    buf_ref[0:CONV_HALO, :] = buf_ref[tt:tt + CONV_HALO, :]
    y = _layernorm(acc, lng_ref[...], lnb_ref[...], LN_EPS)
    o_ref[...] = y * _sigmoid(y)


def _conv(zd, conv_w, conv_b, ln_g, ln_b, batch, tt=512):
    t, two_d = zd.shape
    d_d = two_d // 2
    nt = (t // batch) // tt
    return pl.pallas_call(
        _conv_kernel,
        grid=(batch, nt),
        in_specs=[pl.BlockSpec((tt, two_d), lambda b, i: (b * nt + i, 0)), _full(conv_w.shape),
                  _full((1, d_d)), _full((1, d_d)), _full((1, d_d))],
        out_specs=pl.BlockSpec((tt, d_d), lambda b, i: (b * nt + i, 0)),
        out_shape=jax.ShapeDtypeStruct((t, d_d), F32),
        scratch_shapes=[pltpu.VMEM((tt + CONV_HALO, d_d), F32), pltpu.VMEM((tt + CONV_HALO, d_d), F32)],
        compiler_params=_params("parallel", "arbitrary"),
        name="conv_module",
    )(zd, conv_w, conv_b.reshape(1, d_d), ln_g.reshape(1, d_d), ln_b.reshape(1, d_d))


def _slot_columns(w, per_head, take_lo, take_n):
    rows = w.shape[0]
    wh = w.reshape(rows, C_HEADS, per_head)[:, :, take_lo:take_lo + take_n]
    wh = jnp.pad(wh, ((0, 0), (0, 0), (0, HEAD_SLOT - take_n)))
    return wh.reshape(rows, C_HEADS * HEAD_SLOT)


def _slot_partner(w):
    rows = w.shape[0]
    wh = w.reshape(rows, -1, HEAD_SLOT)
    x1 = wh[:, :, ROPE_LO:ROPE_LO + ROPE_HALF]
    x2 = wh[:, :, ROPE_LO + ROPE_HALF:ROPE_LO + QK_ROPE]
    out = jnp.concatenate([jnp.zeros_like(wh[:, :, :ROPE_LO]), x2, x1,
                           jnp.zeros_like(wh[:, :, ROPE_LO + QK_ROPE:])], axis=2)
    return out.reshape(w.shape)


def _odd_in_weight(w):
    d = w.shape[0]
    lo = Q_LORA + KV_LORA
    kr = jnp.concatenate([jnp.zeros((d, ROPE_LO), w.dtype), w[:, lo:lo + QK_ROPE],
                          jnp.zeros((d, HEAD_SLOT - ROPE_LO - QK_ROPE), w.dtype)], axis=1)
    return jnp.concatenate([w[:, :lo], kr, _slot_partner(kr), w[:, lo + QK_ROPE:]], axis=1)


def kernel(x, norm_ffn1, ffn1_in, ffn1_out, norm_mix, norm_ffn2, ffn2_in, ffn2_out, even_w_in, even_w_out, gsu_ws, gsu_bs, gsu_ln_g, gsu_ln_b, shift_mu, decay_w0, decay_up, iclr_a0, iclr_up, gate_up, k_k, k_a, r_k, lnx_g, lnx_b, odd_w_in, odd_w_out, q_norm, wq_up, kv_norm, wkv_up, conv_w, conv_b, conv_ln_g, conv_ln_b, final_norm):
    batch, seq, d_model = x.shape
    depth = norm_ffn1.shape[0]
    d_a = gsu_ln_g.shape[1]
    d_b = decay_w0.shape[1]
    p_b = shift_mu.shape[1]
    d_d = conv_b.shape[1]
    h = x.reshape(batch * seq, d_model)
    cos_t, sin_t = _rope_tables(seq)
    mixed = None
    ffn1_in, ffn1_out, ffn2_in, ffn2_out, even_w_in, even_w_out, odd_w_out = (
        w.astype(BF16) for w in (ffn1_in, ffn1_out, ffn2_in, ffn2_out, even_w_in, even_w_out, odd_w_out))
    for layer in range(depth):
        w1_in, w1_out = (ffn1_in, layer), (ffn1_out, layer)
        w2_in, w2_out = (ffn2_in, layer), (ffn2_out, layer)
        if layer % 2 == 0:
            e = layer // 2
            h, ya, zb = _ffn(h, norm_ffn1[layer], w1_in, w1_out,
                             post=(norm_mix[layer], (even_w_in, e), (2 * d_a, p_b),
                                   (gsu_ws[e], gsu_bs[e], gsu_ln_g[e], gsu_ln_b[e]), None), name="ffn1_even")
            yb = _rwkv(zb.reshape(batch, seq, p_b), shift_mu[e], decay_w0[e], decay_up[e], iclr_a0[e], iclr_up[e],
                       gate_up[e], k_k[e], k_a[e], r_k[e], lnx_g[e], lnx_b[e]).reshape(batch * seq, d_b)
            mixed = (ya, yb, (even_w_out, e))
        else:
            o = layer // 2
            wq = _slot_columns(wq_up[o], QK_NOPE + QK_ROPE, 0, QK_NOPE + QK_ROPE)
            wk = _slot_columns(wkv_up[o], QK_NOPE + V_HEAD, 0, QK_NOPE).astype(BF16)
            wv = wkv_up[o].reshape(KV_LORA, C_HEADS, QK_NOPE + V_HEAD)[:, :, QK_NOPE:].reshape(
                KV_LORA, C_HEADS * V_HEAD).astype(BF16)
            mla = (cos_t, sin_t, q_norm[o], kv_norm[o], wq.astype(BF16), _slot_partner(wq).astype(BF16), wk, wv, batch)
            h, q, k, v, zd = _ffn(h, norm_ffn1[layer], w1_in, w1_out,
                                  post=(norm_mix[layer], _odd_in_weight(odd_w_in[o]).astype(BF16),
                                        (Q_LORA, KV_LORA, 2 * HEAD_SLOT, 2 * d_d), None, mla), name="ffn1_odd")
            yc = _attention(q, k, v, batch)
            yd = _conv(zd, conv_w[o], conv_b[o], conv_ln_g[o], conv_ln_b[o], batch)
            mixed = (yc, yd, (odd_w_out, o))
        last = layer == depth - 1
        h = _ffn(h, norm_ffn2[layer], w2_in, w2_out, pre=mixed, fin_g=final_norm if last else None,
                 name="ffn2_final" if last else "ffn2")[0]
    return h.reshape(batch, seq, d_model)
```

```python
import functools
import math

import jax
import jax.numpy as jnp
from jax import lax
from jax.experimental import pallas as pl
from jax.experimental.pallas import tpu as pltpu

F32 = jnp.float32
BF16 = jnp.bfloat16

LANES = 128
SUBLANES = 8
VMEM_LIMIT_BYTES = 56 * 1024 * 1024

CHUNK = 64
RMS_EPS = 1e-6
LN_EPS = 1e-5
LNX_EPS = 64e-5
GSU_BLOCK = 128
A_GROUPS = 4
B_HEAD = 64
LORA_W = 64
LORA_A = 64
LORA_G = 128
C_HEADS = 8
Q_LORA = 256
KV_LORA = 128
QK_NOPE = 64
QK_ROPE = 32
V_HEAD = 64
ROPE_THETA = 10000.0
ATTN_SCALE = (QK_NOPE + QK_ROPE) ** -0.5
CONV_K = 31
HEAD_SLOT = 128
ROPE_LO = QK_NOPE
ROPE_HALF = QK_ROPE // 2


def _params(*sem):
    return pltpu.CompilerParams(dimension_semantics=sem, vmem_limit_bytes=VMEM_LIMIT_BYTES)


def _dot(a, b):
    return jnp.dot(a, b, preferred_element_type=F32)


def _dot_nt(a, b):
    return lax.dot_general(a, b, (((1,), (1,)), ((), ())), preferred_element_type=F32)


def _dot_tn(a, b):
    return lax.dot_general(a, b, (((0,), (0,)), ((), ())), preferred_element_type=F32)


def _split_hi_lo(x):
    hi = x.astype(BF16)
    lo = (x - hi.astype(F32)).astype(BF16)
    return hi, lo


def _dot_exact_rhs(x, m):
    hi, lo = _split_hi_lo(x)
    return _dot(hi, m) + _dot(lo, m)


def _dot_exact_lhs(m, x):
    hi, lo = _split_hi_lo(x)
    return _dot(m, hi) + _dot(m, lo)


def _rms(x, g):
    ms = jnp.mean(x * x, axis=-1, keepdims=True)
    return x * lax.rsqrt(ms + RMS_EPS) * g


def _layernorm(x, g, b, eps):
    mu = jnp.mean(x, axis=-1, keepdims=True)
    xc = x - mu
    var = jnp.mean(xc * xc, axis=-1, keepdims=True)
    return xc * lax.rsqrt(var + eps) * g + b


def _sigmoid(x):
    return 1.0 / (1.0 + jnp.exp(-x))


def _full(shape):
    n = len(shape)
    return pl.BlockSpec(shape, lambda *_: (0,) * n)


FFN_CHUNK = 256


MLA_SPLITS = 3


def _ffn_kernel(*refs, d_ff, pre, splits, gsu, mla, final):
    refs = list(refs)
    x_ref = refs.pop(0)
    if pre:
        ya_ref, yb_ref, wmo_ref = refs.pop(0), refs.pop(0), refs.pop(0)
    g_ref, win_ref, wout_ref = refs.pop(0), refs.pop(0), refs.pop(0)
    if splits:
        gm_ref, wmi_ref = refs.pop(0), refs.pop(0)
    if gsu:
        gsu_refs = [refs.pop(0) for _ in range(4)]
    if mla:
        mla_refs = [refs.pop(0) for _ in range(8)]
    if final:
        fin_ref = refs.pop(0)
    o_ref = refs.pop(0)
    if mla:
        mla_refs += [refs.pop(0) for _ in range(3)]
    z_refs = [None] * (MLA_SPLITS if mla else 0)
    z_refs += [refs.pop(0) for _ in range(len(splits) - len(z_refs))]
    (acc_ref,) = refs

    x = x_ref[...]
    if pre:
        da = ya_ref.shape[1]
        x = x + _dot(ya_ref[...].astype(BF16), wmo_ref[:da, :]) + _dot(yb_ref[...].astype(BF16), wmo_ref[da:, :])
    h = _rms(x, g_ref[...]).astype(BF16)
    for j in range(d_ff // FFN_CHUNK):
        lo = j * FFN_CHUNK
        gate = _dot(h, win_ref[:, lo:lo + FFN_CHUNK])
        up = _dot(h, win_ref[:, d_ff + lo:d_ff + lo + FFN_CHUNK])
        act = (gate * _sigmoid(gate) * up).astype(BF16)
        part = _dot(act, wout_ref[lo:lo + FFN_CHUNK, :])
        if j == 0:
            acc_ref[...] = part
        else:
            acc_ref[...] += part
    y = x + 0.5 * acc_ref[...]
    if final:
        y = _rms(y, fin_ref[...])
    o_ref[...] = y
    if splits:
        h2 = _rms(y, gm_ref[...]).astype(BF16)
        lo = 0
        latent = []
        for n, (z_ref, width) in enumerate(zip(z_refs, splits)):
            z = _dot(h2, wmi_ref[:, lo:lo + width])
            if gsu and n == 0:
                _gsu_mix(z, *gsu_refs, z_ref)
            elif z_ref is None:
                latent.append(z)
            else:
                z_ref[...] = z
            lo += width
        if mla:
            _mla_prep(*latent, *mla_refs)


def _weight(w):
    if isinstance(w, tuple):
        stack, layer = w
        return pl.BlockSpec((None,) + stack.shape[1:], lambda *_: (layer, 0, 0)), stack
    return _full(w.shape), w


def _ffn(x, g, w_in, w_out, pre=None, post=None, fin_g=None, tm=512, name="ffn"):
    t, d = x.shape
    rows = lambda w: pl.BlockSpec((tm, w), lambda i: (i, 0))
    in_specs, args = [rows(d)], [x]

    def add_weight(w):
        spec, arr = _weight(w)
        in_specs.append(spec)
        args.append(arr)
        return arr.shape[-2:]

    if pre is not None:
        ya, yb, wmo = pre
        in_specs += [rows(ya.shape[1]), rows(yb.shape[1])]
        args += [ya, yb]
        add_weight(wmo)
    in_specs.append(_full((1, d)))
    args.append(g.reshape(1, d))
    add_weight(w_in)
    d_ff = add_weight(w_out)[0]
    splits, gsu, mla = (), None, None
    if post is not None:
        gm, wmi, splits, gsu, mla = post
        in_specs.append(_full((1, d)))
        args.append(gm.reshape(1, d))
        add_weight(wmi)
    out_widths = list(splits)
    if gsu is not None:
        ws, bs, ln_g, ln_b = gsu
        d_a = splits[0] // 2
        bias = jnp.broadcast_to(bs[:, :, None], (A_GROUPS, GSU_BLOCK, d_a // A_GROUPS))
        in_specs += [_full(ws.shape), _full(bias.shape), _full((1, d_a)), _full((1, d_a))]
        args += [ws, bias, ln_g.reshape(1, d_a), ln_b.reshape(1, d_a)]
        out_widths[0] = d_a
    out_specs = [rows(d)]
    out_shape = [jax.ShapeDtypeStruct((t, d), F32)]
    if mla is not None:
        cos_t, sin_t, q_norm, kv_norm, wq, wqp, wk, wv, batch = mla
        s = t // batch
        nt = s // tm
        tab = pl.BlockSpec((tm, HEAD_SLOT), lambda i: (i % nt, 0))
        in_specs += [tab, tab, _full((1, Q_LORA)), _full((1, KV_LORA)),
                     _full(wq.shape), _full(wqp.shape), _full(wk.shape), _full(wv.shape)]
        args += [cos_t, sin_t, q_norm.reshape(1, -1), kv_norm.reshape(1, -1), wq, wqp, wk, wv]
        qk_spec = pl.BlockSpec((1, C_HEADS, tm, HEAD_SLOT), lambda i: (i // nt, 0, i % nt, 0))
        qk_shape = jax.ShapeDtypeStruct((batch, C_HEADS, s, HEAD_SLOT), BF16)
        out_specs += [qk_spec, qk_spec, rows(C_HEADS * V_HEAD)]
        out_shape += [qk_shape, qk_shape, jax.ShapeDtypeStruct((t, C_HEADS * V_HEAD), BF16)]
        out_widths = out_widths[MLA_SPLITS:]
    if fin_g is not None:
        in_specs.append(_full((1, d)))
        args.append(fin_g.reshape(1, d))
    out_specs += [rows(s) for s in out_widths]
    out_shape += [jax.ShapeDtypeStruct((t, s), F32) for s in out_widths]
    return pl.pallas_call(
        functools.partial(_ffn_kernel, d_ff=d_ff, pre=pre is not None, splits=tuple(splits),
                          gsu=gsu is not None, mla=mla is not None, final=fin_g is not None),
        grid=(t // tm,),
        in_specs=in_specs,
        out_specs=out_specs,
        out_shape=out_shape,
        scratch_shapes=[pltpu.VMEM((tm, d), F32)],
        compiler_params=_params("parallel"),
        name=name,
    )(*args)


def _gelu_tanh(x):
    c = math.sqrt(2.0 / math.pi)
    return 0.5 * x * (1.0 + jnp.tanh(c * (x + 0.044715 * (x * x * x))))


def _gsu_mix(za, ws_ref, bias_ref, lng_ref, lnb_ref, o_ref):
    d_a = o_ref.shape[1]
    gd = d_a // A_GROUPS
    g = _gelu_tanh(za)
    u = g[:, :d_a]
    v = _layernorm(g[:, d_a:], lng_ref[...], lnb_ref[...], LN_EPS).astype(BF16)
    row = lax.broadcasted_iota(jnp.int32, (GSU_BLOCK, GSU_BLOCK), 0) // CHUNK
    col = lax.broadcasted_iota(jnp.int32, (GSU_BLOCK, GSU_BLOCK), 1) // CHUNK
    keep = row >= col
    for gi in range(A_GROUPS):
        w = jnp.where(keep, ws_ref[gi], 0.0).astype(BF16)
        for blk in range(za.shape[0] // GSU_BLOCK):
            rs = slice(blk * GSU_BLOCK, (blk + 1) * GSU_BLOCK)
            cs = slice(gi * gd, (gi + 1) * gd)
            mixed = _dot(w, v[rs, cs]) + bias_ref[gi]
            o_ref[rs, cs] = u[rs, cs] * mixed


RWKV_NB = 4


def _rwkv_kernel(zb_ref, mu_ref, w0_ref, a0_ref, kk_ref, ka_ref, rk_ref, wup_ref, aup_ref, gup_ref,
                 lng_ref, lnb_ref, ones_ref, o_ref, h_ref, buf_ref):
    nb, c, p_b = zb_ref.shape
    d_b = o_ref.shape[2]
    heads = d_b // B_HEAD

    @pl.when(pl.program_id(1) == 0)
    def _():
        h_ref[...] = jnp.zeros(h_ref.shape, F32)
        buf_ref[:, 0:SUBLANES, :] = jnp.zeros((nb, SUBLANES, p_b), F32)

    zbs, prevs = [], []
    for n in range(nb):
        zn = zb_ref[n]
        buf_ref[n, SUBLANES:SUBLANES + c, :] = zn
        prevs.append(buf_ref[n, SUBLANES - 1:SUBLANES - 1 + c, :])
        buf_ref[n, 0:SUBLANES, :] = zn[c - SUBLANES:, :]
        zbs.append(zn)
    zb = jnp.concatenate(zbs, axis=0)
    z = zb + mu_ref[...] * (jnp.concatenate(prevs, axis=0) - zb)

    r = z[:, 0:d_b]
    k = z[:, d_b:2 * d_b]
    v = z[:, 2 * d_b:3 * d_b]
    xwa = z[:, 3 * d_b:3 * d_b + LORA_W + LORA_A]
    xg = z[:, 3 * d_b + LORA_W + LORA_A:]
    lane = lax.broadcasted_iota(jnp.int32, xwa.shape, 1)
    xwa = jnp.where(lane < LORA_W, jnp.tanh(xwa), xwa).astype(BF16)
    ew = _sigmoid(w0_ref[...] + _dot(xwa, wup_ref[...])) * math.exp(-0.5)
    a_lr = _sigmoid(a0_ref[...] + _dot(xwa, aup_ref[...]))
    gate = _dot(_sigmoid(xg).astype(BF16), gup_ref[...])
    ones = ones_ref[...]
    kk = k * kk_ref[...]
    kk = kk / jnp.maximum(jnp.sqrt(_dot_exact_rhs(kk * kk, ones)), 1e-12)
    k = k * (1.0 + (a_lr - 1.0) * ka_ref[...])
    bonus = _dot((r * k * rk_ref[...]).astype(BF16), ones) * v
    a_vec = -kk
    b_vec = kk * a_lr

    ti = lax.broadcasted_iota(jnp.int32, (c, c), 0)
    si = lax.broadcasted_iota(jnp.int32, (c, c), 1)
    tril = (si <= ti).astype(BF16)
    strict = si < ti
    incl = si <= ti
    eye = (ti == si).astype(F32)

    cum = jnp.concatenate([_dot_exact_lhs(tril, ew[n * c:(n + 1) * c]) for n in range(nb)], axis=0)
    total = jnp.concatenate(
        [jnp.broadcast_to(cum[(n + 1) * c - 1:(n + 1) * c], (c, d_b)) for n in range(nb)], axis=0)
    grow = jnp.exp(cum)
    tail = jnp.exp(cum - total)
    a_t = a_vec * jnp.exp(ew - cum)
    r_t = r * jnp.exp(-cum)
    b_t = b_vec * grow
    k_t = k * grow
    b_h = b_vec * tail
    k_h = k * tail
    g_end = [jnp.exp(-cum[(n + 1) * c - 1:(n + 1) * c]) for n in range(nb)]

    bf = lambda t: t.astype(BF16)
    chains = [(n, h) for n in range(nb) for h in range(heads)]
    cr = range(len(chains))
    sel = [(slice(n * c, (n + 1) * c), slice(h * B_HEAD, (h + 1) * B_HEAD)) for n, h in chains]
    cut = lambda t, i: t[sel[i][0], sel[i][1]]
    vh = [bf(cut(v, i)) for i in cr]
    p = [_dot_nt(bf(jnp.concatenate([cut(a_t, i), cut(r_t, i)], axis=0)),
                 bf(jnp.concatenate([cut(b_t, i), cut(k_t, i)], axis=0))) for i in cr]
    nk = [jnp.where(strict, p[i][:c, :c], 0.0) for i in cr]
    a_ak = [bf(jnp.where(strict, p[i][:c, c:], 0.0)) for i in cr]
    m_rb = [bf(jnp.where(incl, p[i][c:, :c], 0.0)) for i in cr]
    m_rk = [bf(jnp.where(incl, p[i][c:, c:], 0.0)) for i in cr]
    x = [jnp.concatenate([cut(a_t, i), _dot(a_ak[i], vh[i])], axis=1) for i in cr]
    y_loc = [_dot(m_rk[i], vh[i]) for i in cr]
    g_loc = [_dot_tn(bf(cut(k_h, i)), vh[i]) for i in cr]
    steps = int(math.log2(c))
    for it in range(steps):
        nkb = [bf(nk[i]) for i in cr]
        x = [x[i] + _dot(nkb[i], bf(x[i])) for i in cr]
        if it + 1 < steps:
            nk = [_dot(nkb[i], nkb[i]) for i in cr]
    xb = [bf(x[i]) for i in cr]
    ry = [_dot(m_rb[i], xb[i]) + jnp.concatenate([cut(r_t, i), y_loc[i]], axis=1) for i in cr]
    pg = [_dot_tn(bf(cut(b_h, i)), xb[i])
          + jnp.concatenate([eye * g_end[chains[i][0]][:, sel[i][1]], g_loc[i]], axis=1) for i in cr]
    h0b = [bf(h_ref[i]) for i in cr]
    ys = [_dot(bf(ry[i][:, :B_HEAD]), h0b[i]) + ry[i][:, B_HEAD:] for i in cr]
    h_new = [_dot(bf(pg[i][:, :B_HEAD]), h0b[i]) + pg[i][:, B_HEAD:] for i in cr]
    for i in cr:
        h_ref[i] = h_new[i]

    y = jnp.concatenate(
        [jnp.concatenate(ys[n * heads:(n + 1) * heads], axis=1) for n in range(nb)], axis=0)
    inv = 1.0 / B_HEAD
    mu = _dot(bf(y), ones) * inv
    yc = y - mu
    var = _dot(bf(yc * yc), ones) * inv
    yn = yc * lax.rsqrt(var + LNX_EPS) * lng_ref[...] + lnb_ref[...]
    out = (yn + bonus) * gate
    for n in range(nb):
        o_ref[n] = out[n * c:(n + 1) * c]


def _head_ones(d, head):
    idx = jnp.arange(d) // head
    return (idx[:, None] == idx[None, :]).astype(BF16)


def _rwkv(zb, shift_mu, decay_w0, decay_up, iclr_a0, iclr_up, gate_up, k_k, k_a, r_k, lnx_g, lnx_b):
    batch, s, p_b = zb.shape
    d_b = decay_w0.shape[0]
    nb = RWKV_NB
    zeros = jnp.zeros((LORA_W, d_b), F32)
    wup = jnp.concatenate([decay_up, zeros], axis=0).astype(BF16)
    aup = jnp.concatenate([zeros, iclr_up], axis=0).astype(BF16)
    row = lambda a: a.reshape(1, -1)
    vec = _full((1, d_b))
    return pl.pallas_call(
        _rwkv_kernel,
        grid=(batch // nb, s // CHUNK),
        in_specs=[pl.BlockSpec((nb, CHUNK, p_b), lambda b, i: (b, i, 0)), _full((1, p_b)),
                  vec, vec, vec, vec, vec, _full(wup.shape), _full(aup.shape), _full((LORA_G, d_b)),
                  vec, vec, _full((d_b, d_b))],
        out_specs=pl.BlockSpec((nb, CHUNK, d_b), lambda b, i: (b, i, 0)),
        out_shape=jax.ShapeDtypeStruct((batch, s, d_b), F32),
        scratch_shapes=[pltpu.VMEM((nb * (d_b // B_HEAD), B_HEAD, B_HEAD), F32),
                        pltpu.VMEM((nb, CHUNK + SUBLANES, p_b), F32)],
        compiler_params=_params("parallel", "arbitrary"),
        name="rwkv",
    )(zb, row(shift_mu), row(decay_w0), row(iclr_a0), row(k_k), row(k_a), row(r_k),
      wup, aup, gate_up.astype(BF16), row(lnx_g), row(lnx_b), _head_ones(d_b, B_HEAD))


def _rope_table_kernel(cos_ref, sin_ref):
    rows = cos_ref.shape[0]
    pos = (pl.program_id(0) * rows + lax.broadcasted_iota(jnp.int32, cos_ref.shape, 0)).astype(F32)
    lane = lax.broadcasted_iota(jnp.int32, cos_ref.shape, 1)
    in_rope = (lane >= ROPE_LO) & (lane < ROPE_LO + QK_ROPE)
    second = lane >= ROPE_LO + ROPE_HALF
    idx = jnp.where(second, lane - ROPE_LO - ROPE_HALF, lane - ROPE_LO).astype(F32)
    inv_freq = jnp.exp(idx * (-2.0 * math.log(ROPE_THETA) / QK_ROPE))
    ang = pos * inv_freq
    cos_ref[...] = jnp.where(in_rope, jnp.cos(ang), 1.0)
    sin_ref[...] = jnp.where(in_rope, jnp.where(second, jnp.sin(ang), -jnp.sin(ang)), 0.0)


def _rope_tables(s, rows=256):
    spec = pl.BlockSpec((rows, HEAD_SLOT), lambda i: (i, 0))
    return pl.pallas_call(
        _rope_table_kernel,
        grid=(s // rows,),
        out_specs=[spec, spec],
        out_shape=[jax.ShapeDtypeStruct((s, HEAD_SLOT), F32)] * 2,
        compiler_params=_params("parallel"),
        name="rope_tables",
    )()


def _mla_prep(cq, ckv, kr, cos_ref, sin_ref, qn_ref, kvn_ref, wq_ref, wqp_ref, wk_ref, wv_ref, q_o, k_o, v_o):
    cos_t = cos_ref[...]
    sin_t = sin_ref[...]
    cqn = _rms(cq, qn_ref[...]).astype(BF16)
    ckvn = _rms(ckv, kvn_ref[...]).astype(BF16)
    q = _dot(cqn, wq_ref[...])
    qp = _dot(cqn, wqp_ref[...])
    kn = _dot(ckvn, wk_ref[...])
    v_o[...] = _dot(ckvn, wv_ref[...]).astype(BF16)
    krot = kr[:, :HEAD_SLOT] * cos_t + kr[:, HEAD_SLOT:] * sin_t
    cos_s = cos_t * ATTN_SCALE
    sin_s = sin_t * ATTN_SCALE
    for h in range(C_HEADS):
        hs = slice(h * HEAD_SLOT, (h + 1) * HEAD_SLOT)
        q_o[0, h] = (q[:, hs] * cos_s + qp[:, hs] * sin_s).astype(BF16)
        k_o[0, h] = (kn[:, hs] + krot).astype(BF16)


ATTN_TQ = 1024


def _attn_kernel(q_ref, k_ref, v_ref, o_ref):
    tq = q_ref.shape[2]
    qi = pl.program_id(2)
    lane = lax.broadcasted_iota(jnp.int32, (tq, 2 * V_HEAD), 1)
    row_c = lax.broadcasted_iota(jnp.int32, (tq, tq), 0) // CHUNK
    col_c = lax.broadcasted_iota(jnp.int32, (tq, tq), 1) // CHUNK
    diag_keep = col_c <= row_c
    pair = range(2)
    q = [q_ref[0, hh] for hh in pair]

    def block(j, carry, masked):
        m, l, acc = carry
        start = pl.multiple_of(j * tq, tq)
        vb = v_ref[pl.ds(start, tq), :]
        sc = [_dot_nt(q[hh], k_ref[0, hh, pl.ds(start, tq), :]) for hh in pair]
        if masked:
            sc = [jnp.where(diag_keep, s, -jnp.inf) for s in sc]
        m_new = [jnp.maximum(m[hh], jnp.max(sc[hh], axis=-1, keepdims=True)) for hh in pair]
        alpha = [jnp.exp(m[hh] - m_new[hh]) for hh in pair]
        p = [jnp.exp(sc[hh] - m_new[hh]) for hh in pair]
        l = [alpha[hh] * l[hh] + jnp.sum(p[hh], axis=-1, keepdims=True) for hh in pair]
        pv = [_dot(p[hh].astype(BF16), vb) for hh in pair]
        acc = [alpha[hh] * acc[hh] + pv[hh] for hh in pair]
        return m_new, l, acc

    init = ([jnp.full((tq, 1), -jnp.inf, F32)] * 2, [jnp.zeros((tq, 1), F32)] * 2,
            [jnp.zeros((tq, 2 * V_HEAD), F32)] * 2)
    carry = lax.fori_loop(0, qi, functools.partial(block, masked=False), init)
    m, l, acc = block(qi, carry, True)
    o_ref[...] = jnp.where(lane < V_HEAD, acc[0] / l[0], acc[1] / l[1])


def _attention(q, k, v, batch):
    _, heads, s, slot = q.shape
    nq = s // ATTN_TQ
    t = batch * s
    return pl.pallas_call(
        _attn_kernel,
        grid=(batch, heads // 2, nq),
        in_specs=[pl.BlockSpec((1, 2, ATTN_TQ, slot), lambda b, p, i: (b, p, i, 0)),
                  pl.BlockSpec((1, 2, s, slot), lambda b, p, i: (b, p, 0, 0)),
                  pl.BlockSpec((s, 2 * V_HEAD), lambda b, p, i: (b, p))],
        out_specs=pl.BlockSpec((ATTN_TQ, 2 * V_HEAD), lambda b, p, i: (b * nq + i, p)),
        out_shape=jax.ShapeDtypeStruct((t, heads * V_HEAD), F32),
        compiler_params=_params("parallel", "parallel", "arbitrary"),
        name="mla_attention",
    )(q, k, v)


CONV_HALO = 32


def _conv_kernel(zd_ref, w_ref, b_ref, lng_ref, lnb_ref, o_ref, buf_ref, shift_ref):
    tt, d_d = o_ref.shape

    @pl.when(pl.program_id(1) == 0)
    def _():
        buf_ref[0:CONV_HALO, :] = jnp.zeros((CONV_HALO, d_d), F32)

    zd = zd_ref[...]
    buf_ref[CONV_HALO:CONV_HALO + tt, :] = zd[:, :d_d] * _sigmoid(zd[:, d_d:])
    acc = jnp.zeros((tt, d_d), F32) + b_ref[...]
    first = CONV_HALO - (CONV_K - 1)
    for rho in range(SUBLANES):
        taps = [j for j in range(CONV_K) if (first + j) % SUBLANES == rho]
        span = first + taps[-1] - rho + tt
        src = buf_ref
        if rho:
            shift_ref[0:span, :] = buf_ref[rho:rho + span, :]
            src = shift_ref
        for j in taps:
            off = first + j - rho
            acc = acc + w_ref[j:j + 1, :] * src[off:off + tt, :]
    buf_ref[0:CONV_HALO, :] = buf_ref[tt:tt + CONV_HALO, :]
    y = _layernorm(acc, lng_ref[...], lnb_ref[...], LN_EPS)
    o_ref[...] = y * _sigmoid(y)


def _conv(zd, conv_w, conv_b, ln_g, ln_b, batch, tt=512):
    t, two_d = zd.shape
    d_d = two_d // 2
    nt = (t // batch) // tt
    return pl.pallas_call(
        _conv_kernel,
        grid=(batch, nt),
        in_specs=[pl.BlockSpec((tt, two_d), lambda b, i: (b * nt + i, 0)), _full(conv_w.shape),
                  _full((1, d_d)), _full((1, d_d)), _full((1, d_d))],
        out_specs=pl.BlockSpec((tt, d_d), lambda b, i: (b * nt + i, 0)),
        out_shape=jax.ShapeDtypeStruct((t, d_d), F32),
        scratch_shapes=[pltpu.VMEM((tt + CONV_HALO, d_d), F32), pltpu.VMEM((tt + CONV_HALO, d_d), F32)],
        compiler_params=_params("parallel", "arbitrary"),
        name="conv_module",
    )(zd, conv_w, conv_b.reshape(1, d_d), ln_g.reshape(1, d_d), ln_b.reshape(1, d_d))


def _slot_columns(w, per_head, take_lo, take_n):
    rows = w.shape[0]
    wh = w.reshape(rows, C_HEADS, per_head)[:, :, take_lo:take_lo + take_n]
    wh = jnp.pad(wh, ((0, 0), (0, 0), (0, HEAD_SLOT - take_n)))
    return wh.reshape(rows, C_HEADS * HEAD_SLOT)


def _slot_partner(w):
    rows = w.shape[0]
    wh = w.reshape(rows, -1, HEAD_SLOT)
    x1 = wh[:, :, ROPE_LO:ROPE_LO + ROPE_HALF]
    x2 = wh[:, :, ROPE_LO + ROPE_HALF:ROPE_LO + QK_ROPE]
    out = jnp.concatenate([jnp.zeros_like(wh[:, :, :ROPE_LO]), x2, x1,
                           jnp.zeros_like(wh[:, :, ROPE_LO + QK_ROPE:])], axis=2)
    return out.reshape(w.shape)


def _odd_in_weight(w):
    d = w.shape[0]
    lo = Q_LORA + KV_LORA
    kr = jnp.concatenate([jnp.zeros((d, ROPE_LO), w.dtype), w[:, lo:lo + QK_ROPE],
                          jnp.zeros((d, HEAD_SLOT - ROPE_LO - QK_ROPE), w.dtype)], axis=1)
    return jnp.concatenate([w[:, :lo], kr, _slot_partner(kr), w[:, lo + QK_ROPE:]], axis=1)


def kernel(x, norm_ffn1, ffn1_in, ffn1_out, norm_mix, norm_ffn2, ffn2_in, ffn2_out, even_w_in, even_w_out, gsu_ws, gsu_bs, gsu_ln_g, gsu_ln_b, shift_mu, decay_w0, decay_up, iclr_a0, iclr_up, gate_up, k_k, k_a, r_k, lnx_g, lnx_b, odd_w_in, odd_w_out, q_norm, wq_up, kv_norm, wkv_up, conv_w, conv_b, conv_ln_g, conv_ln_b, final_norm):
    batch, seq, d_model = x.shape
    depth = norm_ffn1.shape[0]
    d_a = gsu_ln_g.shape[1]
    d_b = decay_w0.shape[1]
    p_b = shift_mu.shape[1]
    d_d = conv_b.shape[1]
    h = x.reshape(batch * seq, d_model)
    cos_t, sin_t = _rope_tables(seq)
    mixed = None
    ffn1_in, ffn1_out, ffn2_in, ffn2_out, even_w_in, even_w_out, odd_w_out = (
        w.astype(BF16) for w in (ffn1_in, ffn1_out, ffn2_in, ffn2_out, even_w_in, even_w_out, odd_w_out))
    for layer in range(depth):
        w1_in, w1_out = (ffn1_in, layer), (ffn1_out, layer)
        w2_in, w2_out = (ffn2_in, layer), (ffn2_out, layer)
        if layer % 2 == 0:
            e = layer // 2
            h, ya, zb = _ffn(h, norm_ffn1[layer], w1_in, w1_out,
                             post=(norm_mix[layer], (even_w_in, e), (2 * d_a, p_b),
                                   (gsu_ws[e], gsu_bs[e], gsu_ln_g[e], gsu_ln_b[e]), None), name="ffn1_even")
            yb = _rwkv(zb.reshape(batch, seq, p_b), shift_mu[e], decay_w0[e], decay_up[e], iclr_a0[e], iclr_up[e],
                       gate_up[e], k_k[e], k_a[e], r_k[e], lnx_g[e], lnx_b[e]).reshape(batch * seq, d_b)
            mixed = (ya, yb, (even_w_out, e))
        else:
            o = layer // 2
            wq = _slot_columns(wq_up[o], QK_NOPE + QK_ROPE, 0, QK_NOPE + QK_ROPE)
            wk = _slot_columns(wkv_up[o], QK_NOPE + V_HEAD, 0, QK_NOPE).astype(BF16)
            wv = wkv_up[o].reshape(KV_LORA, C_HEADS, QK_NOPE + V_HEAD)[:, :, QK_NOPE:].reshape(
                KV_LORA, C_HEADS * V_HEAD).astype(BF16)
            mla = (cos_t, sin_t, q_norm[o], kv_norm[o], wq.astype(BF16), _slot_partner(wq).astype(BF16), wk, wv, batch)
            h, q, k, v, zd = _ffn(h, norm_ffn1[layer], w1_in, w1_out,
                                  post=(norm_mix[layer], _odd_in_weight(odd_w_in[o]).astype(BF16),
                                        (Q_LORA, KV_LORA, 2 * HEAD_SLOT, 2 * d_d), None, mla), name="ffn1_odd")
            yc = _attention(q, k, v, batch)
            yd = _conv(zd, conv_w[o], conv_b[o], conv_ln_g[o], conv_ln_b[o], batch)
            mixed = (yc, yd, (odd_w_out, o))
        last = layer == depth - 1
        h = _ffn(h, norm_ffn2[layer], w2_in, w2_out, pre=mixed, fin_g=final_norm if last else None,
                 name="ffn2_final" if last else "ffn2")[0]
    return h.reshape(batch, seq, d_model)
```

```python
import functools
import math

import jax
import jax.numpy as jnp
from jax import lax
from jax.experimental import pallas as pl
from jax.experimental.pallas import tpu as pltpu

F32 = jnp.float32
BF16 = jnp.bfloat16

LANES = 128
SUBLANES = 8
VMEM_LIMIT_BYTES = 56 * 1024 * 1024

CHUNK = 64
RMS_EPS = 1e-6
LN_EPS = 1e-5
LNX_EPS = 64e-5
GSU_BLOCK = 128
A_GROUPS = 4
B_HEAD = 64
LORA_W = 64
LORA_A = 64
LORA_G = 128
C_HEADS = 8
Q_LORA = 256
KV_LORA = 128
QK_NOPE = 64
QK_ROPE = 32
V_HEAD = 64
ROPE_THETA = 10000.0
ATTN_SCALE = (QK_NOPE + QK_ROPE) ** -0.5
CONV_K = 31
HEAD_SLOT = 128
ROPE_LO = QK_NOPE
ROPE_HALF = QK_ROPE // 2


def _params(*sem):
    return pltpu.CompilerParams(dimension_semantics=sem, vmem_limit_bytes=VMEM_LIMIT_BYTES)


def _dot(a, b):
    return jnp.dot(a, b, preferred_element_type=F32)


def _dot_nt(a, b):
    return lax.dot_general(a, b, (((1,), (1,)), ((), ())), preferred_element_type=F32)


def _dot_tn(a, b):
    return lax.dot_general(a, b, (((0,), (0,)), ((), ())), preferred_element_type=F32)


def _split_hi_lo(x):
    hi = x.astype(BF16)
    lo = (x - hi.astype(F32)).astype(BF16)
    return hi, lo


def _dot_exact_rhs(x, m):
    hi, lo = _split_hi_lo(x)
    return _dot(hi, m) + _dot(lo, m)


def _dot_exact_lhs(m, x):
    hi, lo = _split_hi_lo(x)
    return _dot(m, hi) + _dot(m, lo)


def _rms(x, g):
    ms = jnp.mean(x * x, axis=-1, keepdims=True)
    return x * lax.rsqrt(ms + RMS_EPS) * g


def _layernorm(x, g, b, eps):
    mu = jnp.mean(x, axis=-1, keepdims=True)
    xc = x - mu
    var = jnp.mean(xc * xc, axis=-1, keepdims=True)
    return xc * lax.rsqrt(var + eps) * g + b


def _sigmoid(x):
    return 1.0 / (1.0 + jnp.exp(-x))


def _full(shape):
    n = len(shape)
    return pl.BlockSpec(shape, lambda *_: (0,) * n)


FFN_CHUNK = 256


MLA_SPLITS = 3


def _ffn_kernel(*refs, d_ff, pre, splits, gsu, mla, final):
    refs = list(refs)
    x_ref = refs.pop(0)
    if pre:
        ya_ref, yb_ref, wmo_ref = refs.pop(0), refs.pop(0), refs.pop(0)
    g_ref, win_ref, wout_ref = refs.pop(0), refs.pop(0), refs.pop(0)
    if splits:
        gm_ref, wmi_ref = refs.pop(0), refs.pop(0)
    if gsu:
        gsu_refs = [refs.pop(0) for _ in range(4)]
    if mla:
        mla_refs = [refs.pop(0) for _ in range(8)]
    if final:
        fin_ref = refs.pop(0)
    o_ref = refs.pop(0)
    if mla:
        mla_refs += [refs.pop(0) for _ in range(3)]
    z_refs = [None] * (MLA_SPLITS if mla else 0)
    z_refs += [refs.pop(0) for _ in range(len(splits) - len(z_refs))]
    (acc_ref,) = refs

    x = x_ref[...]
    if pre:
        da = ya_ref.shape[1]
        x = x + _dot(ya_ref[...].astype(BF16), wmo_ref[:da, :]) + _dot(yb_ref[...].astype(BF16), wmo_ref[da:, :])
    h = _rms(x, g_ref[...]).astype(BF16)
    for j in range(d_ff // FFN_CHUNK):
        lo = j * FFN_CHUNK
        gate = _dot(h, win_ref[:, lo:lo + FFN_CHUNK])
        up = _dot(h, win_ref[:, d_ff + lo:d_ff + lo + FFN_CHUNK])
        act = (gate * _sigmoid(gate) * up).astype(BF16)
        part = _dot(act, wout_ref[lo:lo + FFN_CHUNK, :])
        if j == 0:
            acc_ref[...] = part
        else:
            acc_ref[...] += part
    y = x + 0.5 * acc_ref[...]
    if final:
        y = _rms(y, fin_ref[...])
    o_ref[...] = y
    if splits:
        h2 = _rms(y, gm_ref[...]).astype(BF16)
        lo = 0
        latent = []
        for n, (z_ref, width) in enumerate(zip(z_refs, splits)):
            z = _dot(h2, wmi_ref[:, lo:lo + width])
            if gsu and n == 0:
                _gsu_mix(z, *gsu_refs, z_ref)
            elif z_ref is None:
                latent.append(z)
            else:
                z_ref[...] = z
            lo += width
        if mla:
            _mla_prep(*latent, *mla_refs)


def _weight(w):
    if isinstance(w, tuple):
        stack, layer = w
        return pl.BlockSpec((None,) + stack.shape[1:], lambda *_: (layer, 0, 0)), stack
    return _full(w.shape), w


def _ffn(x, g, w_in, w_out, pre=None, post=None, fin_g=None, tm=512, name="ffn"):
    t, d = x.shape
    rows = lambda w: pl.BlockSpec((tm, w), lambda i: (i, 0))
    in_specs, args = [rows(d)], [x]

    def add_weight(w):
        spec, arr = _weight(w)
        in_specs.append(spec)
        args.append(arr)
        return arr.shape[-2:]

    if pre is not None:
        ya, yb, wmo = pre
        in_specs += [rows(ya.shape[1]), rows(yb.shape[1])]
        args += [ya, yb]
        add_weight(wmo)
    in_specs.append(_full((1, d)))
    args.append(g.reshape(1, d))
    add_weight(w_in)
    d_ff = add_weight(w_out)[0]
    splits, gsu, mla = (), None, None
    if post is not None:
        gm, wmi, splits, gsu, mla = post
        in_specs.append(_full((1, d)))
        args.append(gm.reshape(1, d))
        add_weight(wmi)
    out_widths = list(splits)
    if gsu is not None:
        ws, bs, ln_g, ln_b = gsu
        d_a = splits[0] // 2
        bias = jnp.broadcast_to(bs[:, :, None], (A_GROUPS, GSU_BLOCK, d_a // A_GROUPS))
        in_specs += [_full(ws.shape), _full(bias.shape), _full((1, d_a)), _full((1, d_a))]
        args += [ws, bias, ln_g.reshape(1, d_a), ln_b.reshape(1, d_a)]
        out_widths[0] = d_a
    out_specs = [rows(d)]
    out_shape = [jax.ShapeDtypeStruct((t, d), F32)]
    if mla is not None:
        cos_t, sin_t, q_norm, kv_norm, wq, wqp, wk, wv, batch = mla
        s = t // batch
        nt = s // tm
        tab = pl.BlockSpec((tm, HEAD_SLOT), lambda i: (i % nt, 0))
        in_specs += [tab, tab, _full((1, Q_LORA)), _full((1, KV_LORA)),
                     _full(wq.shape), _full(wqp.shape), _full(wk.shape), _full(wv.shape)]
        args += [cos_t, sin_t, q_norm.reshape(1, -1), kv_norm.reshape(1, -1), wq, wqp, wk, wv]
        qk_spec = pl.BlockSpec((1, C_HEADS, tm, HEAD_SLOT), lambda i: (i // nt, 0, i % nt, 0))
        qk_shape = jax.ShapeDtypeStruct((batch, C_HEADS, s, HEAD_SLOT), BF16)
        out_specs += [qk_spec, qk_spec, rows(C_HEADS * HEAD_SLOT)]
        out_shape += [qk_shape, qk_shape, jax.ShapeDtypeStruct((t, C_HEADS * HEAD_SLOT), BF16)]
        out_widths = out_widths[MLA_SPLITS:]
    if fin_g is not None:
        in_specs.append(_full((1, d)))
        args.append(fin_g.reshape(1, d))
    out_specs += [rows(s) for s in out_widths]
    out_shape += [jax.ShapeDtypeStruct((t, s), F32) for s in out_widths]
    return pl.pallas_call(
        functools.partial(_ffn_kernel, d_ff=d_ff, pre=pre is not None, splits=tuple(splits),
                          gsu=gsu is not None, mla=mla is not None, final=fin_g is not None),
        grid=(t // tm,),
        in_specs=in_specs,
        out_specs=out_specs,
        out_shape=out_shape,
        scratch_shapes=[pltpu.VMEM((tm, d), F32)],
        compiler_params=_params("parallel"),
        name=name,
    )(*args)


def _gelu_tanh(x):
    c = math.sqrt(2.0 / math.pi)
    return 0.5 * x * (1.0 + jnp.tanh(c * (x + 0.044715 * (x * x * x))))


def _gsu_mix(za, ws_ref, bias_ref, lng_ref, lnb_ref, o_ref):
    d_a = o_ref.shape[1]
    gd = d_a // A_GROUPS
    g = _gelu_tanh(za)
    u = g[:, :d_a]
    v = _layernorm(g[:, d_a:], lng_ref[...], lnb_ref[...], LN_EPS).astype(BF16)
    row = lax.broadcasted_iota(jnp.int32, (GSU_BLOCK, GSU_BLOCK), 0) // CHUNK
    col = lax.broadcasted_iota(jnp.int32, (GSU_BLOCK, GSU_BLOCK), 1) // CHUNK
    keep = row >= col
    for gi in range(A_GROUPS):
        w = jnp.where(keep, ws_ref[gi], 0.0).astype(BF16)
        for blk in range(za.shape[0] // GSU_BLOCK):
            rs = slice(blk * GSU_BLOCK, (blk + 1) * GSU_BLOCK)
            cs = slice(gi * gd, (gi + 1) * gd)
            mixed = _dot(w, v[rs, cs]) + bias_ref[gi]
            o_ref[rs, cs] = u[rs, cs] * mixed


RWKV_NB = 4


def _rwkv_kernel(zb_ref, mu_ref, w0_ref, a0_ref, kk_ref, ka_ref, rk_ref, wup_ref, aup_ref, gup_ref,
                 lng_ref, lnb_ref, ones_ref, o_ref, h_ref, buf_ref):
    nb, c, p_b = zb_ref.shape
    d_b = o_ref.shape[2]
    heads = d_b // B_HEAD

    @pl.when(pl.program_id(1) == 0)
    def _():
        h_ref[...] = jnp.zeros(h_ref.shape, F32)
        buf_ref[:, 0:SUBLANES, :] = jnp.zeros((nb, SUBLANES, p_b), F32)

    zbs, prevs = [], []
    for n in range(nb):
        zn = zb_ref[n]
        buf_ref[n, SUBLANES:SUBLANES + c, :] = zn
        prevs.append(buf_ref[n, SUBLANES - 1:SUBLANES - 1 + c, :])
        buf_ref[n, 0:SUBLANES, :] = zn[c - SUBLANES:, :]
        zbs.append(zn)
    zb = jnp.concatenate(zbs, axis=0)
    z = zb + mu_ref[...] * (jnp.concatenate(prevs, axis=0) - zb)

    r = z[:, 0:d_b]
    k = z[:, d_b:2 * d_b]
    v = z[:, 2 * d_b:3 * d_b]
    xwa = z[:, 3 * d_b:3 * d_b + LORA_W + LORA_A]
    xg = z[:, 3 * d_b + LORA_W + LORA_A:]
    lane = lax.broadcasted_iota(jnp.int32, xwa.shape, 1)
    xwa = jnp.where(lane < LORA_W, jnp.tanh(xwa), xwa).astype(BF16)
    ew = _sigmoid(w0_ref[...] + _dot(xwa, wup_ref[...])) * math.exp(-0.5)
    a_lr = _sigmoid(a0_ref[...] + _dot(xwa, aup_ref[...]))
    gate = _dot(_sigmoid(xg).astype(BF16), gup_ref[...])
    ones = ones_ref[...]
    kk = k * kk_ref[...]
    kk = kk / jnp.maximum(jnp.sqrt(_dot_exact_rhs(kk * kk, ones)), 1e-12)
    k = k * (1.0 + (a_lr - 1.0) * ka_ref[...])
    bonus = _dot((r * k * rk_ref[...]).astype(BF16), ones) * v
    a_vec = -kk
    b_vec = kk * a_lr

    ti = lax.broadcasted_iota(jnp.int32, (c, c), 0)
    si = lax.broadcasted_iota(jnp.int32, (c, c), 1)
    tril = (si <= ti).astype(BF16)
    strict = si < ti
    incl = si <= ti
    eye = (ti == si).astype(F32)

    cum = jnp.concatenate([_dot_exact_lhs(tril, ew[n * c:(n + 1) * c]) for n in range(nb)], axis=0)
    total = jnp.concatenate(
        [jnp.broadcast_to(cum[(n + 1) * c - 1:(n + 1) * c], (c, d_b)) for n in range(nb)], axis=0)
    grow = jnp.exp(cum)
    tail = jnp.exp(cum - total)
    a_t = a_vec * jnp.exp(ew - cum)
    r_t = r * jnp.exp(-cum)
    b_t = b_vec * grow
    k_t = k * grow
    b_h = b_vec * tail
    k_h = k * tail
    g_end = [jnp.exp(-cum[(n + 1) * c - 1:(n + 1) * c]) for n in range(nb)]

    bf = lambda t: t.astype(BF16)
    chains = [(n, h) for n in range(nb) for h in range(heads)]
    cr = range(len(chains))
    sel = [(slice(n * c, (n + 1) * c), slice(h * B_HEAD, (h + 1) * B_HEAD)) for n, h in chains]
    cut = lambda t, i: t[sel[i][0], sel[i][1]]
    vh = [bf(cut(v, i)) for i in cr]
    p = [_dot_nt(bf(jnp.concatenate([cut(a_t, i), cut(r_t, i)], axis=0)),
                 bf(jnp.concatenate([cut(b_t, i), cut(k_t, i)], axis=0))) for i in cr]
    nk = [jnp.where(strict, p[i][:c, :c], 0.0) for i in cr]
    a_ak = [bf(jnp.where(strict, p[i][:c, c:], 0.0)) for i in cr]
    m_rb = [bf(jnp.where(incl, p[i][c:, :c], 0.0)) for i in cr]
    m_rk = [bf(jnp.where(incl, p[i][c:, c:], 0.0)) for i in cr]
    x = [jnp.concatenate([cut(a_t, i), _dot(a_ak[i], vh[i])], axis=1) for i in cr]
    y_loc = [_dot(m_rk[i], vh[i]) for i in cr]
    g_loc = [_dot_tn(bf(cut(k_h, i)), vh[i]) for i in cr]
    steps = int(math.log2(c))
    for it in range(steps):
        nkb = [bf(nk[i]) for i in cr]
        x = [x[i] + _dot(nkb[i], bf(x[i])) for i in cr]
        if it + 1 < steps:
            nk = [_dot(nkb[i], nkb[i]) for i in cr]
    xb = [bf(x[i]) for i in cr]
    ry = [_dot(m_rb[i], xb[i]) + jnp.concatenate([cut(r_t, i), y_loc[i]], axis=1) for i in cr]
    pg = [_dot_tn(bf(cut(b_h, i)), xb[i])
          + jnp.concatenate([eye * g_end[chains[i][0]][:, sel[i][1]], g_loc[i]], axis=1) for i in cr]
    h0b = [bf(h_ref[i]) for i in cr]
    ys = [_dot(bf(ry[i][:, :B_HEAD]), h0b[i]) + ry[i][:, B_HEAD:] for i in cr]
    h_new = [_dot(bf(pg[i][:, :B_HEAD]), h0b[i]) + pg[i][:, B_HEAD:] for i in cr]
    for i in cr:
        h_ref[i] = h_new[i]

    y = jnp.concatenate(
        [jnp.concatenate(ys[n * heads:(n + 1) * heads], axis=1) for n in range(nb)], axis=0)
    inv = 1.0 / B_HEAD
    mu = _dot(bf(y), ones) * inv
    yc = y - mu
    var = _dot(bf(yc * yc), ones) * inv
    yn = yc * lax.rsqrt(var + LNX_EPS) * lng_ref[...] + lnb_ref[...]
    out = (yn + bonus) * gate
    for n in range(nb):
        o_ref[n] = out[n * c:(n + 1) * c]


def _head_ones(d, head):
    idx = jnp.arange(d) // head
    return (idx[:, None] == idx[None, :]).astype(BF16)


def _rwkv(zb, shift_mu, decay_w0, decay_up, iclr_a0, iclr_up, gate_up, k_k, k_a, r_k, lnx_g, lnx_b):
    batch, s, p_b = zb.shape
    d_b = decay_w0.shape[0]
    nb = RWKV_NB
    zeros = jnp.zeros((LORA_W, d_b), F32)
    wup = jnp.concatenate([decay_up, zeros], axis=0).astype(BF16)
    aup = jnp.concatenate([zeros, iclr_up], axis=0).astype(BF16)
    row = lambda a: a.reshape(1, -1)
    vec = _full((1, d_b))
    return pl.pallas_call(
        _rwkv_kernel,
        grid=(batch // nb, s // CHUNK),
        in_specs=[pl.BlockSpec((nb, CHUNK, p_b), lambda b, i: (b, i, 0)), _full((1, p_b)),
                  vec, vec, vec, vec, vec, _full(wup.shape), _full(aup.shape), _full((LORA_G, d_b)),
                  vec, vec, _full((d_b, d_b))],
        out_specs=pl.BlockSpec((nb, CHUNK, d_b), lambda b, i: (b, i, 0)),
        out_shape=jax.ShapeDtypeStruct((batch, s, d_b), F32),
        scratch_shapes=[pltpu.VMEM((nb * (d_b // B_HEAD), B_HEAD, B_HEAD), F32),
                        pltpu.VMEM((nb, CHUNK + SUBLANES, p_b), F32)],
        compiler_params=_params("parallel", "arbitrary"),
        name="rwkv",
    )(zb, row(shift_mu), row(decay_w0), row(iclr_a0), row(k_k), row(k_a), row(r_k),
      wup, aup, gate_up.astype(BF16), row(lnx_g), row(lnx_b), _head_ones(d_b, B_HEAD))


def _rope_table_kernel(cos_ref, sin_ref):
    rows = cos_ref.shape[0]
    pos = (pl.program_id(0) * rows + lax.broadcasted_iota(jnp.int32, cos_ref.shape, 0)).astype(F32)
    lane = lax.broadcasted_iota(jnp.int32, cos_ref.shape, 1)
    in_rope = (lane >= ROPE_LO) & (lane < ROPE_LO + QK_ROPE)
    second = lane >= ROPE_LO + ROPE_HALF
    idx = jnp.where(second, lane - ROPE_LO - ROPE_HALF, lane - ROPE_LO).astype(F32)
    inv_freq = jnp.exp(idx * (-2.0 * math.log(ROPE_THETA) / QK_ROPE))
    ang = pos * inv_freq
    cos_ref[...] = jnp.where(in_rope, jnp.cos(ang), 1.0)
    sin_ref[...] = jnp.where(in_rope, jnp.where(second, jnp.sin(ang), -jnp.sin(ang)), 0.0)


def _rope_tables(s, rows=256):
    spec = pl.BlockSpec((rows, HEAD_SLOT), lambda i: (i, 0))
    return pl.pallas_call(
        _rope_table_kernel,
        grid=(s // rows,),
        out_specs=[spec, spec],
        out_shape=[jax.ShapeDtypeStruct((s, HEAD_SLOT), F32)] * 2,
        compiler_params=_params("parallel"),
        name="rope_tables",
    )()


def _mla_prep(cq, ckv, kr, cos_ref, sin_ref, qn_ref, kvn_ref, wq_ref, wqp_ref, wk_ref, wv_ref, q_o, k_o, v_o):
    cos_t = cos_ref[...]
    sin_t = sin_ref[...]
    cqn = _rms(cq, qn_ref[...]).astype(BF16)
    ckvn = _rms(ckv, kvn_ref[...]).astype(BF16)
    q = _dot(cqn, wq_ref[...])
    qp = _dot(cqn, wqp_ref[...])
    kn = _dot(ckvn, wk_ref[...])
    v_lane = lax.broadcasted_iota(jnp.int32, (1, v_o.shape[1]), 1) % HEAD_SLOT
    v_o[...] = (_dot(ckvn, wv_ref[...]) + (v_lane >= V_HEAD).astype(F32)).astype(BF16)
    krot = kr[:, :HEAD_SLOT] * cos_t + kr[:, HEAD_SLOT:] * sin_t
    cos_s = cos_t * ATTN_SCALE
    sin_s = sin_t * ATTN_SCALE
    for h in range(C_HEADS):
        hs = slice(h * HEAD_SLOT, (h + 1) * HEAD_SLOT)
        q_o[0, h] = (q[:, hs] * cos_s + qp[:, hs] * sin_s).astype(BF16)
        k_o[0, h] = (kn[:, hs] + krot).astype(BF16)


ATTN_TQ = 1024


def _attn_kernel(q_ref, k_ref, v_ref, o_ref):
    tq = q_ref.shape[2]
    qi = pl.program_id(2)
    lane = lax.broadcasted_iota(jnp.int32, (tq, 2 * V_HEAD), 1)
    row_c = lax.broadcasted_iota(jnp.int32, (tq, tq), 0) // CHUNK
    col_c = lax.broadcasted_iota(jnp.int32, (tq, tq), 1) // CHUNK
    diag_keep = col_c <= row_c
    pair = range(2)
    q = [q_ref[0, hh] for hh in pair]

    def block(j, carry, masked):
        m, acc = carry
        start = pl.multiple_of(j * tq, tq)
        vb = v_ref[pl.ds(start, tq), :]
        sc = [_dot_nt(q[hh], k_ref[0, hh, pl.ds(start, tq), :]) for hh in pair]
        if masked:
            sc = [jnp.where(diag_keep, s, -jnp.inf) for s in sc]
        m_new = [jnp.maximum(m[hh], jnp.max(sc[hh], axis=-1, keepdims=True)) for hh in pair]
        alpha = [jnp.exp(m[hh] - m_new[hh]) for hh in pair]
        p = [jnp.exp(sc[hh] - m_new[hh]) for hh in pair]
        pv = [_dot(p[hh].astype(BF16), vb[:, hh * HEAD_SLOT:(hh + 1) * HEAD_SLOT]) for hh in pair]
        acc = [alpha[hh] * acc[hh] + pv[hh] for hh in pair]
        return m_new, acc

    init = ([jnp.full((tq, 1), -jnp.inf, F32)] * 2, [jnp.zeros((tq, HEAD_SLOT), F32)] * 2)
    carry = lax.fori_loop(0, qi, functools.partial(block, masked=False), init)
    m, acc = block(qi, carry, True)
    o_ref[...] = jnp.where(lane < V_HEAD, acc[0] / pltpu.roll(acc[0], V_HEAD, 1),
                           pltpu.roll(acc[1], V_HEAD, 1) / acc[1])


def _attention(q, k, v, batch):
    _, heads, s, slot = q.shape
    nq = s // ATTN_TQ
    t = batch * s
    return pl.pallas_call(
        _attn_kernel,
        grid=(batch, heads // 2, nq),
        in_specs=[pl.BlockSpec((1, 2, ATTN_TQ, slot), lambda b, p, i: (b, p, i, 0)),
                  pl.BlockSpec((1, 2, s, slot), lambda b, p, i: (b, p, 0, 0)),
                  pl.BlockSpec((s, 2 * HEAD_SLOT), lambda b, p, i: (b, p))],
        out_specs=pl.BlockSpec((ATTN_TQ, 2 * V_HEAD), lambda b, p, i: (b * nq + i, p)),
        out_shape=jax.ShapeDtypeStruct((t, heads * V_HEAD), F32),
        compiler_params=_params("parallel", "parallel", "arbitrary"),
        name="mla_attention",
    )(q, k, v)


CONV_HALO = 32


def _conv_kernel(zd_ref, w_ref, b_ref, lng_ref, lnb_ref, o_ref, buf_ref, shift_ref):
    tt, d_d = o_ref.shape

    @pl.when(pl.program_id(1) == 0)
    def _():
        buf_ref[0:CONV_HALO, :] = jnp.zeros((CONV_HALO, d_d), F32)

    zd = zd_ref[...]
    buf_ref[CONV_HALO:CONV_HALO + tt, :] = zd[:, :d_d] * _sigmoid(zd[:, d_d:])
    acc = jnp.zeros((tt, d_d), F32) + b_ref[...]
    first = CONV_HALO - (CONV_K - 1)
    for rho in range(SUBLANES):
        taps = [j for j in range(CONV_K) if (first + j) % SUBLANES == rho]
        span = first + taps[-1] - rho + tt
        src = buf_ref
        if rho:
            shift_ref[0:span, :] = buf_ref[rho:rho + span, :]
            src = shift_ref
        for j in taps:
            off = first + j - rho
            acc = acc + w_ref[j:j + 1, :] * src[off:off + tt, :]
    buf_ref[0:CONV_HALO, :] = buf_ref[tt:tt + CONV_HALO, :]
    y = _layernorm(acc, lng_ref[...], lnb_ref[...], LN_EPS)
    o_ref[...] = y * _sigmoid(y)


def _conv(zd, conv_w, conv_b, ln_g, ln_b, batch, tt=512):
    t, two_d = zd.shape
    d_d = two_d // 2
    nt = (t // batch) // tt
    return pl.pallas_call(
        _conv_kernel,
        grid=(batch, nt),
        in_specs=[pl.BlockSpec((tt, two_d), lambda b, i: (b * nt + i, 0)), _full(conv_w.shape),
                  _full((1, d_d)), _full((1, d_d)), _full((1, d_d))],
        out_specs=pl.BlockSpec((tt, d_d), lambda b, i: (b * nt + i, 0)),
        out_shape=jax.ShapeDtypeStruct((t, d_d), F32),
        scratch_shapes=[pltpu.VMEM((tt + CONV_HALO, d_d), F32), pltpu.VMEM((tt + CONV_HALO, d_d), F32)],
        compiler_params=_params("parallel", "arbitrary"),
        name="conv_module",
    )(zd, conv_w, conv_b.reshape(1, d_d), ln_g.reshape(1, d_d), ln_b.reshape(1, d_d))


def _slot_columns(w, per_head, take_lo, take_n):
    rows = w.shape[0]
    wh = w.reshape(rows, C_HEADS, per_head)[:, :, take_lo:take_lo + take_n]
    wh = jnp.pad(wh, ((0, 0), (0, 0), (0, HEAD_SLOT - take_n)))
    return wh.reshape(rows, C_HEADS * HEAD_SLOT)


def _slot_partner(w):
    rows = w.shape[0]
    wh = w.reshape(rows, -1, HEAD_SLOT)
    x1 = wh[:, :, ROPE_LO:ROPE_LO + ROPE_HALF]
    x2 = wh[:, :, ROPE_LO + ROPE_HALF:ROPE_LO + QK_ROPE]
    out = jnp.concatenate([jnp.zeros_like(wh[:, :, :ROPE_LO]), x2, x1,
                           jnp.zeros_like(wh[:, :, ROPE_LO + QK_ROPE:])], axis=2)
    return out.reshape(w.shape)


def _odd_in_weight(w):
    d = w.shape[0]
    lo = Q_LORA + KV_LORA
    kr = jnp.concatenate([jnp.zeros((d, ROPE_LO), w.dtype), w[:, lo:lo + QK_ROPE],
                          jnp.zeros((d, HEAD_SLOT - ROPE_LO - QK_ROPE), w.dtype)], axis=1)
    return jnp.concatenate([w[:, :lo], kr, _slot_partner(kr), w[:, lo + QK_ROPE:]], axis=1)


def kernel(x, norm_ffn1, ffn1_in, ffn1_out, norm_mix, norm_ffn2, ffn2_in, ffn2_out, even_w_in, even_w_out, gsu_ws, gsu_bs, gsu_ln_g, gsu_ln_b, shift_mu, decay_w0, decay_up, iclr_a0, iclr_up, gate_up, k_k, k_a, r_k, lnx_g, lnx_b, odd_w_in, odd_w_out, q_norm, wq_up, kv_norm, wkv_up, conv_w, conv_b, conv_ln_g, conv_ln_b, final_norm):
    batch, seq, d_model = x.shape
    depth = norm_ffn1.shape[0]
    d_a = gsu_ln_g.shape[1]
    d_b = decay_w0.shape[1]
    p_b = shift_mu.shape[1]
    d_d = conv_b.shape[1]
    h = x.reshape(batch * seq, d_model)
    cos_t, sin_t = _rope_tables(seq)
    mixed = None
    ffn1_in, ffn1_out, ffn2_in, ffn2_out, even_w_in, even_w_out, odd_w_out = (
        w.astype(BF16) for w in (ffn1_in, ffn1_out, ffn2_in, ffn2_out, even_w_in, even_w_out, odd_w_out))
    for layer in range(depth):
        w1_in, w1_out = (ffn1_in, layer), (ffn1_out, layer)
        w2_in, w2_out = (ffn2_in, layer), (ffn2_out, layer)
        if layer % 2 == 0:
            e = layer // 2
            h, ya, zb = _ffn(h, norm_ffn1[layer], w1_in, w1_out,
                             post=(norm_mix[layer], (even_w_in, e), (2 * d_a, p_b),
                                   (gsu_ws[e], gsu_bs[e], gsu_ln_g[e], gsu_ln_b[e]), None), name="ffn1_even")
            yb = _rwkv(zb.reshape(batch, seq, p_b), shift_mu[e], decay_w0[e], decay_up[e], iclr_a0[e], iclr_up[e],
                       gate_up[e], k_k[e], k_a[e], r_k[e], lnx_g[e], lnx_b[e]).reshape(batch * seq, d_b)
            mixed = (ya, yb, (even_w_out, e))
        else:
            o = layer // 2
            wq = _slot_columns(wq_up[o], QK_NOPE + QK_ROPE, 0, QK_NOPE + QK_ROPE)
            wk = _slot_columns(wkv_up[o], QK_NOPE + V_HEAD, 0, QK_NOPE).astype(BF16)
            wv = _slot_columns(wkv_up[o], QK_NOPE + V_HEAD, QK_NOPE, V_HEAD).astype(BF16)
            mla = (cos_t, sin_t, q_norm[o], kv_norm[o], wq.astype(BF16), _slot_partner(wq).astype(BF16), wk, wv, batch)
            h, q, k, v, zd = _ffn(h, norm_ffn1[layer], w1_in, w1_out,
                                  post=(norm_mix[layer], _odd_in_weight(odd_w_in[o]).astype(BF16),
                                        (Q_LORA, KV_LORA, 2 * HEAD_SLOT, 2 * d_d), None, mla), name="ffn1_odd")
            yc = _attention(q, k, v, batch)
            yd = _conv(zd, conv_w[o], conv_b[o], conv_ln_g[o], conv_ln_b[o], batch)
            mixed = (yc, yd, (odd_w_out, o))
        last = layer == depth - 1
        h = _ffn(h, norm_ffn2[layer], w2_in, w2_out, pre=mixed, fin_g=final_norm if last else None,
                 name="ffn2_final" if last else "ffn2")[0]
    return h.reshape(batch, seq, d_model)
```
